```python
import jax, jax.numpy as jnp
from jax import lax
import numpy as np

D_MODEL = 1024
BATCH = 32
SEQ = 256
DEPTH = 4
DEC_BATCH = 2
DEC_SEQ = 4096
PAST_LEN = 256

GRID_W = 64
N_EVEN = (DEPTH + 1) // 2
N_ODD = DEPTH // 2
HD = 64
A_HEADS = 8
A_KV = 2
A_W = A_HEADS * HD
B_HEADS = 4
B_DK = 64
B_DV = 128
B_RANK = 16
B_W = B_HEADS * B_DV
GLA_NORMALIZER = 16.0
C_HEADS = 8
C_KV = 2
C_W = C_HEADS * HD
WINDOW = 128
D_HEADS = 4
D_DK = 128
D_DV = 128
D_W = D_HEADS * D_DV
CONV_K = 5
CONV_CH = 2 * D_HEADS * D_DK + D_W
MIX_W = A_W + B_W
Q_BLOCK = 128
CHUNK = 64
ROPE_THETA = 10000.0
ROT_AXIS = HD // 2
EPS = 1e-6
EVEN_SIZES = (A_W, A_KV * HD, A_KV * HD, A_W, B_HEADS * B_DK, B_HEADS * B_DK, B_W, 2 * B_RANK, B_W)
ODD_SIZES = (C_W, C_KV * HD, C_KV * HD, C_W, D_HEADS * D_DK, D_HEADS * D_DK, D_W, 2 * D_HEADS, 2 * D_HEADS, D_W)
PROJ_EVEN = 2 * A_W + 2 * A_KV * HD + 2 * B_HEADS * B_DK + 2 * B_W + 2 * B_RANK
PROJ_ODD = 2 * C_W + 2 * C_KV * HD + 2 * D_HEADS * D_DK + 2 * D_W + 4 * D_HEADS

kernel_name = 'hybrid_flow_backbone_step'

F32 = jnp.float32


def split_cols(z, sizes):
    idx = [int(i) for i in np.cumsum(sizes)[:-1]]
    return jnp.split(z, idx, axis=-1)


def rms_norm(x, w):
    xf = x.astype(F32)
    y = xf * lax.rsqrt(jnp.mean(xf * xf, axis=-1, keepdims=True) + EPS)
    return (y * w.astype(F32)).astype(x.dtype)


def l2_norm(x):
    xf = x.astype(F32)
    return (xf * lax.rsqrt(jnp.sum(xf * xf, axis=-1, keepdims=True) + EPS)).astype(x.dtype)


def flip(x):
    return x[:, ::-1]


def adaln_in(x, cond, w, b, g):
    m = jax.nn.silu(cond) @ w + b
    shift, scale, gate = jnp.split(m[:, None, :], 3, axis=-1)
    return rms_norm(x, g) * (1 + scale) + shift, gate


def rope_tables(T):
    n_rows = T // GRID_W
    rows = jnp.repeat(jnp.arange(n_rows, dtype=F32), GRID_W)
    cols = jnp.tile(jnp.arange(GRID_W, dtype=F32), n_rows)
    inv = jnp.power(ROPE_THETA, jnp.arange(ROT_AXIS // 2, dtype=F32) * (-2.0 / ROT_AXIS))
    ang_r = rows[:, None] * inv[None, :]
    ang_c = cols[:, None] * inv[None, :]
    return (jnp.cos(ang_r), jnp.sin(ang_r), jnp.cos(ang_c), jnp.sin(ang_c))


def rope_1d(x, cos, sin):
    half = x.shape[-1] // 2
    xf = x.astype(F32)
    x1, x2 = xf[..., :half], xf[..., half:]
    cc, ss = cos[None, :, None, :], sin[None, :, None, :]
    return jnp.concatenate([x1 * cc - x2 * ss, x2 * cc + x1 * ss], axis=-1)


def axial_rope(x, tables):
    cr, sr, cc, sc = tables
    out = jnp.concatenate([rope_1d(x[..., :ROT_AXIS], cr, sr), rope_1d(x[..., ROT_AXIS:], cc, sc)], axis=-1)
    return out.astype(x.dtype)


def block_attention(q, k, v, sink=None):
    B, Tq, H, hd = q.shape
    KV = k.shape[2]
    G = H // KV
    nb = Tq // Q_BLOCK
    qb = q.reshape(B, nb, Q_BLOCK, KV, G, hd).swapaxes(0, 1)
    scale = hd ** -0.5

    def one(qblk):
        s = jnp.einsum('bqkgd,bskd->bkgqs', qblk, k).astype(F32) * scale
        if sink is not None:
            sk = jnp.broadcast_to(sink.astype(F32).reshape(KV, G)[None, :, :, None, None], s.shape[:-1] + (1,))
            p = jax.nn.softmax(jnp.concatenate([s, sk], axis=-1), axis=-1)[..., :-1]
        else:
            p = jax.nn.softmax(s, axis=-1)
        return jnp.einsum('bkgqs,bskd->bqkgd', p.astype(v.dtype), v)

    o = lax.map(one, qb)
    return o.swapaxes(0, 1).reshape(B, Tq, H, hd)


def banded_attention(q, k, v, k_ctx, v_ctx, sink):
    B, T, H, hd = q.shape
    KV = k.shape[2]
    G = H // KV
    nb = T // Q_BLOCK
    nl = 3 * Q_BLOCK
    nc = k_ctx.shape[1]
    pad = ((0, 0), (Q_BLOCK, Q_BLOCK), (0, 0), (0, 0))
    k_pad = jnp.pad(k, pad)
    v_pad = jnp.pad(v, pad)
    qb = q.reshape(B, nb, Q_BLOCK, KV, G, hd).swapaxes(0, 1)
    a_idx = jnp.arange(Q_BLOCK)[:, None]
    b_idx = jnp.arange(nl)[None, :]
    dist = Q_BLOCK + a_idx - b_idx
    sink_col = sink.astype(F32).reshape(KV, G)[None, :, :, None, None]
    scale = hd ** -0.5

    def one(args):
        qblk, n = args
        kb = lax.dynamic_slice_in_dim(k_pad, n * Q_BLOCK, nl, axis=1)
        vb = lax.dynamic_slice_in_dim(v_pad, n * Q_BLOCK, nl, axis=1)
        j = n * Q_BLOCK - Q_BLOCK + b_idx
        valid = (jnp.abs(dist) <= WINDOW) & (j >= 0) & (j < T)
        s_loc = jnp.einsum('bqkgd,bskd->bkgqs', qblk, kb).astype(F32) * scale
        s_loc = jnp.where(valid, s_loc, -jnp.inf)
        s_ctx = jnp.einsum('bqkgd,bskd->bkgqs', qblk, k_ctx).astype(F32) * scale
        s = jnp.concatenate([s_loc, s_ctx, jnp.broadcast_to(sink_col, s_loc.shape[:-1] + (1,))], axis=-1)
        p = jax.nn.softmax(s, axis=-1).astype(v.dtype)
        return (jnp.einsum('bkgqs,bskd->bqkgd', p[..., :nl], vb)
                + jnp.einsum('bkgqs,bskd->bqkgd', p[..., nl:nl + nc], v_ctx))

    o = lax.map(one, (qb, jnp.arange(nb)))
    return o.swapaxes(0, 1).reshape(B, T, H, hd)


def gla_chunked(q, k, v, g, s0):
    B, T, H, dk = q.shape
    dv = v.shape[-1]
    N = T // CHUNK
    qc = q.astype(F32).reshape(B, N, CHUNK, H, dk)
    kc = k.astype(F32).reshape(B, N, CHUNK, H, dk)
    vc = v.astype(F32).reshape(B, N, CHUNK, H, dv)
    b = jnp.cumsum(g.astype(F32).reshape(B, N, CHUNK, H, dk), axis=2)
    b_last = b[:, :, -1]
    q_t = qc * jnp.exp(b)
    k_t = kc * jnp.exp(-b)
    k_end = kc * jnp.exp(b_last[:, :, None] - b)
    incl = jnp.tril(jnp.ones((CHUNK, CHUNK), dtype=bool))
    att = jnp.where(incl, jnp.einsum('bnihd,bnjhd->bnhij', q_t, k_t), 0.0)
    o_intra = jnp.einsum('bnhij,bnjhv->bnihv', att, vc)
    u = jnp.einsum('bnjhd,bnjhv->bnhdv', k_end, vc)

    def step(s, inp):
        dec, un = inp
        return dec[..., None] * s + un, s

    s_fin, s_prev = lax.scan(step, s0.astype(F32), (jnp.exp(b_last).swapaxes(0, 1), u.swapaxes(0, 1)))
    o_inter = jnp.einsum('bnihd,bnhdv->bnihv', q_t, s_prev.swapaxes(0, 1))
    o = (o_intra + o_inter).reshape(B, T, H, dv)
    return o.astype(v.dtype), s_fin.astype(v.dtype)


def delta_chunked(q, k, v, g, beta, s0):
    B, T, H, dk = q.shape
    dv = v.shape[-1]
    N = T // CHUNK

    def chunk(x):
        return x.astype(F32).reshape((B, N, CHUNK) + x.shape[2:]).swapaxes(2, 3)

    qc, kc, vc, bc = chunk(q), chunk(k), chunk(v), chunk(beta)
    gc = jnp.cumsum(chunk(g), axis=-1)
    idx = jnp.arange(CHUNK)
    incl = idx[:, None] >= idx[None, :]
    strict = idx[:, None] > idx[None, :]
    decay = jnp.exp(jnp.where(incl, gc[..., :, None] - gc[..., None, :], -jnp.inf))
    kb = kc * bc[..., None]
    t_mat = jnp.where(strict, jnp.einsum('bnhid,bnhjd->bnhij', kb, kc) * decay, 0.0)
    rhs = jnp.concatenate([vc * bc[..., None], kb * jnp.exp(gc)[..., None]], axis=-1)
    sol = lax.linalg.triangular_solve(jnp.eye(CHUNK, dtype=F32) + t_mat, rhs, left_side=True, lower=True)
    u, w = sol[..., :dv], sol[..., dv:]
    att = jnp.einsum('bnhid,bnhjd->bnhij', qc, kc) * decay
    q_dec = qc * jnp.exp(gc)[..., None]
    g_last = gc[..., -1]
    k_dec = kc * jnp.exp(g_last[..., None] - gc)[..., None]

    def step(s, inp):
        u_n, w_n, att_n, qd_n, kd_n, gl_n = inp
        v_new = u_n - jnp.einsum('bhcd,bhdv->bhcv', w_n, s)
        o_n = jnp.einsum('bhcd,bhdv->bhcv', qd_n, s) + jnp.einsum('bhij,bhjv->bhiv', att_n, v_new)
        s = s * jnp.exp(gl_n)[..., None, None] + jnp.einsum('bhcd,bhcv->bhdv', kd_n, v_new)
        return s, o_n

    xs = (u.swapaxes(0, 1), w.swapaxes(0, 1), att.swapaxes(0, 1), q_dec.swapaxes(0, 1),
          k_dec.swapaxes(0, 1), g_last.swapaxes(0, 1))
    s_fin, o = lax.scan(step, s0.astype(F32), xs)
    o = o.transpose(1, 0, 3, 2, 4).reshape(B, T, H, dv)
    return o.astype(v.dtype), s_fin.astype(v.dtype)


def gla_log_gate(r, up, bias):
    B, T = r.shape[:2]
    gk = r @ up + bias
    return jax.nn.log_sigmoid(gk.astype(F32)).reshape(B, T, B_HEADS, B_DK) / GLA_NORMALIZER


def delta_log_decay(a, a_log, dt_bias):
    return -jnp.exp(a_log.astype(F32)) * jax.nn.softplus(a.astype(F32) + dt_bias.astype(F32))


def short_conv(x, w):
    y = lax.conv_general_dilated(x, w[:, None, :].astype(x.dtype), window_strides=(1,),
                                 padding=[(CONV_K // 2, CONV_K // 2)],
                                 dimension_numbers=('NWC', 'WIO', 'NWC'), feature_group_count=x.shape[-1])
    return jax.nn.silu(y)


def even_mixer(h, w_in, w_out, qn, kn, gk_up, gk_bias, onorm, rope=None, ctx=None):
    B, T, _ = h.shape
    a_q, a_k, a_v, a_g, b_q, b_k, b_v, b_r, b_g = split_cols(h @ w_in, EVEN_SIZES)
    q = rms_norm(a_q.reshape(B, T, A_HEADS, HD), qn)
    k = rms_norm(a_k.reshape(B, T, A_KV, HD), kn)
    v = a_v.reshape(B, T, A_KV, HD)
    if ctx is None:
        o_a = block_attention(q, k, v)
    else:
        k_ctx, v_ctx, s_ctx = ctx
        o_a = block_attention(axial_rope(q, rope),
                              jnp.concatenate([axial_rope(k, rope), k_ctx], axis=1),
                              jnp.concatenate([v, v_ctx], axis=1))
    y_a = o_a.reshape(B, T, A_W) * jax.nn.silu(a_g)
    qg = b_q.reshape(B, T, B_HEADS, B_DK) * (B_DK ** -0.5)
    kg = b_k.reshape(B, T, B_HEADS, B_DK)
    vg = b_v.reshape(B, T, B_HEADS, B_DV)
    r_f, r_b = jnp.split(b_r, 2, axis=-1)
    g_f = gla_log_gate(r_f, gk_up[0], gk_bias[0])
    g_b = gla_log_gate(r_b, gk_up[1], gk_bias[1])
    if ctx is None:
        s0_f = s0_b = jnp.zeros((B, B_HEADS, B_DK, B_DV), h.dtype)
    else:
        s0_f, s0_b = s_ctx[:, 0], s_ctx[:, 1]
    o_f, s_f = gla_chunked(qg, kg, vg, g_f, s0_f)
    o_b, s_b = gla_chunked(flip(qg), flip(kg), flip(vg), flip(g_b), s0_b)
    y_b = rms_norm(o_f + flip(o_b), onorm).reshape(B, T, B_W) * jax.nn.silu(b_g)
    out = jnp.concatenate([y_a, y_b], axis=-1) @ w_out
    if ctx is None:
        return out, (k, v, jnp.stack([s_f, s_b], axis=1))
    return out


def odd_mixer(h, w_in, w_out, sink, conv_w, a_log, dt_bias, onorm, rope=None, ctx=None):
    B, T, _ = h.shape
    c_q, c_k, c_v, c_g, d_q, d_k, d_v, d_a, d_b, d_g = split_cols(h @ w_in, ODD_SIZES)
    q = c_q.reshape(B, T, C_HEADS, HD)
    k = c_k.reshape(B, T, C_KV, HD)
    v = c_v.reshape(B, T, C_KV, HD)
    if ctx is None:
        o_c = block_attention(q, k, v, sink)
    else:
        k_ctx, v_ctx, s_ctx = ctx
        o_c = banded_attention(axial_rope(q, rope), axial_rope(k, rope), v, k_ctx, v_ctx, sink)
    y_c = o_c.reshape(B, T, C_W) * jax.nn.silu(c_g)
    qkv = short_conv(jnp.concatenate([d_q, d_k, d_v], axis=-1), conv_w)
    qd, kd, vd = split_cols(qkv, (D_HEADS * D_DK, D_HEADS * D_DK, D_W))
    qd = l2_norm(qd.reshape(B, T, D_HEADS, D_DK)) * (D_DK ** -0.5)
    kd = l2_norm(kd.reshape(B, T, D_HEADS, D_DK))
    vd = vd.reshape(B, T, D_HEADS, D_DV)
    a_f, a_b = jnp.split(d_a, 2, axis=-1)
    b_f, b_b = jnp.split(d_b, 2, axis=-1)
    g_f = delta_log_decay(a_f, a_log[0], dt_bias[0])
    g_b = delta_log_decay(a_b, a_log[1], dt_bias[1])
    beta_f = jax.nn.sigmoid(b_f.astype(F32))
    beta_b = jax.nn.sigmoid(b_b.astype(F32))
    if ctx is None:
        s0_f = s0_b = jnp.zeros((B, D_HEADS, D_DK, D_DV), h.dtype)
    else:
        s0_f, s0_b = s_ctx[:, 0], s_ctx[:, 1]
    o_f, s_f = delta_chunked(qd, kd, vd, g_f, beta_f, s0_f)
    o_b, s_b = delta_chunked(flip(qd), flip(kd), flip(vd), flip(g_b), flip(beta_b), s0_b)
    y_d = rms_norm(o_f + flip(o_b), onorm).reshape(B, T, D_W) * jax.nn.silu(d_g)
    out = jnp.concatenate([y_c, y_d], axis=-1) @ w_out
    if ctx is None:
        return out, (k, v, jnp.stack([s_f, s_b], axis=1))
    return out


def setup_inputs(seed: int = 0) -> dict:
    key = jax.random.key(seed)
    ks = jax.random.split(key, 32)

    def nrm(k, shape, scale):
        return jax.random.normal(k, shape, F32) * scale

    u_a = jax.random.uniform(ks[27], (N_ODD, 2, D_HEADS), F32, 1.0, 16.0)
    dt = jnp.exp(jax.random.uniform(ks[28], (N_ODD, 2, D_HEADS), F32, np.log(1e-3), np.log(1e-1)))
    return {
        'x_prompt': nrm(ks[0], (BATCH, SEQ, D_MODEL), 1.0),
        'x_sample': nrm(ks[1], (DEC_BATCH, DEC_SEQ, D_MODEL), 1.0),
        'c': nrm(ks[2], (DEC_BATCH, D_MODEL), 1.0),
        'cache_attn_k': nrm(ks[3], (DEC_BATCH, N_EVEN, PAST_LEN, A_KV, HD), 1.0),
        'cache_attn_v': nrm(ks[4], (DEC_BATCH, N_EVEN, PAST_LEN, A_KV, HD), 1.0),
        'state_gla': nrm(ks[5], (DEC_BATCH, N_EVEN, 2, B_HEADS, B_DK, B_DV), 0.3),
        'cache_swa_k': nrm(ks[6], (DEC_BATCH, N_ODD, PAST_LEN, C_KV, HD), 1.0),
        'cache_swa_v': nrm(ks[7], (DEC_BATCH, N_ODD, PAST_LEN, C_KV, HD), 1.0),
        'state_delta': nrm(ks[8], (DEC_BATCH, N_ODD, 2, D_HEADS, D_DK, D_DV), 0.3),
        'c_ctx': nrm(ks[9], (D_MODEL,), 1.0),
        'w_mod': nrm(ks[10], (DEPTH, D_MODEL, 3 * D_MODEL), 0.5 * D_MODEL ** -0.5),
        'b_mod': nrm(ks[11], (DEPTH, 3 * D_MODEL), 0.02),
        'g_pre': 1.0 + nrm(ks[12], (DEPTH, D_MODEL), 0.02),
        'g_post': 1.0 + nrm(ks[13], (DEPTH, D_MODEL), 0.02),
        'w_in_even': nrm(ks[14], (N_EVEN, D_MODEL, PROJ_EVEN), D_MODEL ** -0.5),
        'w_out_even': nrm(ks[15], (N_EVEN, MIX_W, D_MODEL), MIX_W ** -0.5),
        'qnorm_a': 1.0 + nrm(ks[16], (N_EVEN, HD), 0.02),
        'knorm_a': 1.0 + nrm(ks[17], (N_EVEN, HD), 0.02),
        'gla_gk_up': nrm(ks[18], (N_EVEN, 2, B_RANK, B_HEADS * B_DK), B_RANK ** -0.5),
        'gla_gk_bias': nrm(ks[19], (N_EVEN, 2, B_HEADS * B_DK), 0.1),
        'gla_onorm': 1.0 + nrm(ks[20], (N_EVEN, B_DV), 0.02),
        'w_in_odd': nrm(ks[21], (N_ODD, D_MODEL, PROJ_ODD), D_MODEL ** -0.5),
        'w_out_odd': nrm(ks[22], (N_ODD, MIX_W, D_MODEL), MIX_W ** -0.5),
        'sink_c': nrm(ks[23], (N_ODD, C_HEADS), 0.5),
        'conv_d': nrm(ks[24], (N_ODD, CONV_K, CONV_CH), CONV_K ** -0.5),
        'a_log_d': jnp.log(u_a),
        'dt_bias_d': dt + jnp.log(-jnp.expm1(-dt)),
        'delta_onorm': 1.0 + nrm(ks[25], (N_ODD, D_DV), 0.02),
    }


def reference(x_prompt, x_sample, c, cache_attn_k, cache_attn_v, state_gla, cache_swa_k, cache_swa_v,
              state_delta, c_ctx, w_mod, b_mod, g_pre, g_post, w_in_even, w_out_even, qnorm_a, knorm_a,
              gla_gk_up, gla_gk_bias, gla_onorm, w_in_odd, w_out_odd, sink_c, conv_d, a_log_d, dt_bias_d,
              delta_onorm):
    rope = rope_tables(x_sample.shape[1])
    y_p, y_s = x_prompt, x_sample
    attn_k, attn_v, gla_s, swa_k, swa_v, delta_s = [], [], [], [], [], []
    for l in range(DEPTH):
        i = l // 2
        h_p, gate_p = adaln_in(y_p, c_ctx[None, :], w_mod[l], b_mod[l], g_pre[l])
        h_s, gate_s = adaln_in(y_s, c, w_mod[l], b_mod[l], g_pre[l])
        if l % 2 == 0:
            prm = (w_in_even[i], w_out_even[i], qnorm_a[i], knorm_a[i], gla_gk_up[i], gla_gk_bias[i], gla_onorm[i])
            out_p, (k_c, v_c, s_c) = even_mixer(h_p, *prm)
            out_s = even_mixer(h_s, *prm, rope=rope,
                               ctx=(cache_attn_k[:, i], cache_attn_v[:, i], state_gla[:, i]))
            attn_k.append(k_c)
            attn_v.append(v_c)
            gla_s.append(s_c)
        else:
            prm = (w_in_odd[i], w_out_odd[i], sink_c[i], conv_d[i], a_log_d[i], dt_bias_d[i], delta_onorm[i])
            out_p, (k_c, v_c, s_c) = odd_mixer(h_p, *prm)
            out_s = odd_mixer(h_s, *prm, rope=rope,
                              ctx=(cache_swa_k[:, i], cache_swa_v[:, i], state_delta[:, i]))
            swa_k.append(k_c)
            swa_v.append(v_c)
            delta_s.append(s_c)
        y_p = y_p + gate_p * rms_norm(out_p, g_post[l])
        y_s = y_s + gate_s * rms_norm(out_s, g_post[l])
    new_attn_k = jnp.stack(attn_k, axis=1)
    new_attn_v = jnp.stack(attn_v, axis=1)
    new_gla_state = jnp.stack(gla_s, axis=1)
    new_swa_k = jnp.stack(swa_k, axis=1)
    new_swa_v = jnp.stack(swa_v, axis=1)
    new_delta_state = jnp.stack(delta_s, axis=1)
    return (y_p, y_s, new_attn_k, new_attn_v, new_gla_state, new_swa_k, new_swa_v, new_delta_state)
```

```python
import functools

import numpy as np
import jax
import jax.numpy as jnp
from jax import lax
from jax.experimental import pallas as pl
from jax.experimental.pallas import tpu as pltpu

F32 = jnp.float32
BF16 = jnp.bfloat16

D_MODEL = 1024
DEPTH = 4
GRID_W = 64
HD = 64
N_QH = 8
N_KV = 2
QW = N_QH * HD
KVW = N_KV * HD
B_HEADS = 4
B_DK = 64
B_DV = 128
B_RANK = 16
GLA_NORMALIZER = 16.0
D_HEADS = 4
D_DK = 128
D_DV = 128
CONV_K = 5
WINDOW = 128
Q_BLOCK = 128
CHUNK = 64
ROPE_THETA = 10000.0
ROT_AXIS = HD // 2
EPS = 1e-6
LANE = 128
HALO = 8

ROW_TILE = 256
REC_TILE = 256
VMEM_LIMIT = 48 * 1024 * 1024

E_Q, E_G, E_BV, E_BG, E_BQ, E_BK, E_K, E_V, E_R, E_W = 0, 512, 1024, 1536, 2048, 2304, 2560, 2688, 2816, 2944
O_DQKV, O_Q, O_G, O_DG, O_K, O_V, O_AB, O_W = 0, 1536, 2048, 2560, 3072, 3200, 3328, 3456


def _sigmoid(x):
    return 1.0 / (1.0 + jnp.exp(-x))


def _silu(x):
    return x * _sigmoid(x)


def _softplus(x):
    return jnp.maximum(x, 0.0) + jnp.log(1.0 + jnp.exp(-jnp.abs(x)))


def _dot(a, b):
    return jnp.dot(a.astype(BF16), b.astype(BF16), preferred_element_type=F32)


def _dot_nt(a, b):
    return lax.dot_general(a.astype(BF16), b.astype(BF16), (((1,), (1,)), ((), ())),
                           preferred_element_type=F32)


def _dot_tn(a, b):
    return lax.dot_general(a.astype(BF16), b.astype(BF16), (((0,), (0,)), ((), ())),
                           preferred_element_type=F32)


def _split(x):
    hi = x.astype(BF16)
    lo = (x - hi.astype(F32)).astype(BF16)
    return hi, lo


def _dot_exact_lhs(m, x):
    hi, lo = _split(x)
    return _dot(m, hi) + _dot(m, lo)


def _dot_tn_exact_rhs(x, m):
    hi, lo = _split(x)
    return _dot_tn(hi, m) + _dot_tn(lo, m)


def _dot3(a, b):
    ah, al = _split(a)
    bh, bl = _split(b)
    return _dot(ah, bh) + (_dot(ah, bl) + _dot(al, bh))


def _iota(shape, dim):
    return lax.broadcasted_iota(jnp.int32, shape, dim)


def _cparams(sem):
    return pltpu.CompilerParams(dimension_semantics=sem, vmem_limit_bytes=VMEM_LIMIT)


def _mod_kernel(c_ref, w_ref, b_ref, o_ref):
    cs = _silu(c_ref[...])
    o_ref[0] = _dot(cs, w_ref[0]) + b_ref[0]


def _modulation(cond, w_mod, b_mod):
    tn = 1024
    return pl.pallas_call(
        _mod_kernel,
        grid=(DEPTH, 3 * D_MODEL // tn),
        in_specs=[
            pl.BlockSpec((8, D_MODEL), lambda l, j: (0, 0)),
            pl.BlockSpec((1, D_MODEL, tn), lambda l, j: (l, 0, j)),
            pl.BlockSpec((1, 1, tn), lambda l, j: (l, 0, j)),
        ],
        out_specs=pl.BlockSpec((1, 8, tn), lambda l, j: (l, 0, j)),
        out_shape=jax.ShapeDtypeStruct((DEPTH, 8, 3 * D_MODEL), F32),
        compiler_params=_cparams(("parallel", "parallel")),
        name="modulation",
    )(cond, w_mod, b_mod.reshape(DEPTH, 1, 3 * D_MODEL))


def _head_rms(x, bd, w):
    ss = _dot_exact_lhs_right(x * x, bd)
    return x * lax.rsqrt(ss * (1.0 / HD) + EPS) * w


def _dot_exact_lhs_right(x, m):
    hi, lo = _split(x)
    return _dot(hi, m) + _dot(lo, m)


def _rope(x, cos, sa, sb):
    return x * cos + pltpu.roll(x, LANE - ROT_AXIS // 2, axis=1) * sa + pltpu.roll(x, ROT_AXIS // 2, axis=1) * sb


def _inproj_kernel(*refs, even, rope):
    x_ref, shift_ref, scale_ref, g_ref, w_ref = refs[:5]
    pos = 5
    if even:
        qn_ref, kn_ref, bd_ref = refs[pos:pos + 3]
        pos += 3
    if rope:
        cos_ref, sa_ref, sb_ref = refs[pos:pos + 3]
        pos += 3
    o_ref = refs[pos]

    x = x_ref[...]
    y = x * lax.rsqrt(jnp.mean(x * x, axis=-1, keepdims=True) + EPS) * g_ref[...]
    h = (y * (1.0 + scale_ref[0]) + shift_ref[0]).astype(BF16)

    q_off, k_off = (E_Q, E_K) if even else (O_Q, O_K)
    width = E_W if even else O_W
    special = [q_off + c * LANE for c in range(QW // LANE)] + [k_off]

    def transform(z, is_q):
        if even:
            z = _head_rms(z, bd_ref[...], qn_ref[...] if is_q else kn_ref[...])
        if rope:
            z = _rope(z, cos_ref[...], sa_ref[...], sb_ref[...])
        return z

    if even or rope:
        for off in special:
            z = jnp.dot(h, w_ref[:, off:off + LANE], preferred_element_type=F32)
            o_ref[:, off:off + LANE] = transform(z, off != k_off)
        edges = sorted(special)
        spans, cur = [], 0
        for off in edges:
            if off > cur:
                spans.append((cur, off))
            cur = off + LANE
        if cur < width:
            spans.append((cur, width))
    else:
        spans = [(0, width)]
    for a, b in spans:
        step = 512
        for s in range(a, b, step):
            e = min(s + step, b)
            o_ref[:, s:e] = jnp.dot(h, w_ref[:, s:e], preferred_element_type=F32)


def _inproj(x, shift, scale, g_pre, w, *, even, rope_tabs=None, qn=None, kn=None, bd=None, rows_per_batch):
    T = x.shape[0]
    width = w.shape[1]
    tpb = rows_per_batch // ROW_TILE
    rope = rope_tabs is not None
    per_batch = shift.shape[0] > 1
    mod_idx = (lambda i: (i // tpb, 0, 0)) if per_batch else (lambda i: (0, 0, 0))
    ins = [x, shift, scale, g_pre.reshape(1, D_MODEL), w]
    specs = [
        pl.BlockSpec((ROW_TILE, D_MODEL), lambda i: (i, 0)),
        pl.BlockSpec((1, 1, D_MODEL), mod_idx),
        pl.BlockSpec((1, 1, D_MODEL), mod_idx),
        pl.BlockSpec((1, D_MODEL), lambda i: (0, 0)),
        pl.BlockSpec((D_MODEL, width), lambda i: (0, 0)),
    ]
    if even:
        ins += [qn, kn, bd]
        specs += [pl.BlockSpec((1, LANE), lambda i: (0, 0)),
                  pl.BlockSpec((1, LANE), lambda i: (0, 0)),
                  pl.BlockSpec((LANE, LANE), lambda i: (0, 0))]
    if rope:
        ins += list(rope_tabs)
        specs += [pl.BlockSpec((ROW_TILE, LANE), lambda i: (i % tpb, 0))] * 3
    return pl.pallas_call(
        functools.partial(_inproj_kernel, even=even, rope=rope),
        grid=(T // ROW_TILE,),
        in_specs=specs,
        out_specs=pl.BlockSpec((ROW_TILE, width), lambda i: (i, 0)),
        out_shape=jax.ShapeDtypeStruct((T, width), F32),
        compiler_params=_cparams(("parallel",)),
        name="inproj_even" if even else "inproj_odd",
    )(*ins)


def _attn_dense_kernel(*refs, has_ctx, has_sink, tq):
    q_ref, k_ref, v_ref, g_ref = refs[:4]
    pos = 4
    if has_ctx:
        kc_ref, vc_ref = refs[pos:pos + 2]
        pos += 2
    if has_sink:
        sink_ref = refs[pos]
        pos += 1
    o_ref, kb_ref, vb_ref = refs[pos:pos + 3]

    @pl.when(pl.program_id(1) == 0)
    def _():
        kb_ref[...] = k_ref[...].astype(BF16)
        vb_ref[...] = v_ref[...].astype(BF16)

    kb = kb_ref[...]
    vb = vb_ref[...]
    if has_ctx:
        kcb = kc_ref[0].astype(BF16)
        vcb = vc_ref[0].astype(BF16)
    lower = _iota((tq, LANE), 1) < HD
    scale = HD ** -0.5
    for c in range(QW // LANE):
        qc = q_ref[:, c * LANE:(c + 1) * LANE]
        halves = []
        for e in range(N_KV):
            qe = jnp.where(lower if e == 0 else jnp.logical_not(lower), qc, 0.0).astype(BF16)
            s = _dot_nt(qe, kb) * scale
            m = jnp.max(s, axis=-1, keepdims=True)
            if has_ctx:
                sc = _dot_nt(qe, kcb) * scale
                m = jnp.maximum(m, jnp.max(sc, axis=-1, keepdims=True))
            if has_sink:
                sk = sink_ref[c * N_KV + e]
                m = jnp.maximum(m, sk)
            p = jnp.exp(s - m)
            l = jnp.sum(p, axis=-1, keepdims=True)
            o = _dot(p, vb)
            if has_ctx:
                pc = jnp.exp(sc - m)
                l = l + jnp.sum(pc, axis=-1, keepdims=True)
                o = o + _dot(pc, vcb)
            if has_sink:
                l = l + jnp.exp(sk - m)
            halves.append(o / l)
        oc = jnp.where(lower, halves[0], halves[1])
        o_ref[:, c * LANE:(c + 1) * LANE] = oc * _silu(g_ref[:, c * LANE:(c + 1) * LANE])


def _attn_dense(z, *, n_batch, seq, tq, q_off, g_off, k_off, v_off, ctx=None, sink=None):
    T = z.shape[0]
    nq = seq // tq
    has_ctx = ctx is not None
    has_sink = sink is not None
    ins = [z, z, z, z]
    specs = [
        pl.BlockSpec((tq, QW), lambda b, i: (b * nq + i, q_off // QW)),
        pl.BlockSpec((seq, KVW), lambda b, i: (b, k_off // KVW)),
        pl.BlockSpec((seq, KVW), lambda b, i: (b, v_off // KVW)),
        pl.BlockSpec((tq, QW), lambda b, i: (b * nq + i, g_off // QW)),
    ]
    if has_ctx:
        ins += list(ctx)
        n_ctx = ctx[0].shape[1]
        specs += [pl.BlockSpec((1, n_ctx, KVW), lambda b, i: (b, 0, 0))] * 2
    if has_sink:
        ins.append(sink)
        specs.append(pl.BlockSpec(memory_space=pltpu.SMEM))
    return pl.pallas_call(
        functools.partial(_attn_dense_kernel, has_ctx=has_ctx, has_sink=has_sink, tq=tq),
        grid=(n_batch, nq),
        in_specs=specs,
        out_specs=pl.BlockSpec((tq, QW), lambda b, i: (b * nq + i, 0)),
        out_shape=jax.ShapeDtypeStruct((T, QW), F32),
        scratch_shapes=[pltpu.VMEM((seq, KVW), BF16), pltpu.VMEM((seq, KVW), BF16)],
        compiler_params=_cparams(("arbitrary", "arbitrary")),
        name="attn_dense",
    )(*ins)


def _attn_band_kernel(q_ref, kp_ref, kc_ref, kn_ref, vp_ref, vc_ref, vn_ref, g_ref, kx_ref, vx_ref, sink_ref,
                      o_ref, *, seq):
    n = pl.program_id(1)
    tq = Q_BLOCK
    kcat = jnp.concatenate([kp_ref[...], kc_ref[...], kn_ref[...]], axis=0).astype(BF16)
    vcat = jnp.concatenate([vp_ref[...], vc_ref[...], vn_ref[...]], axis=0).astype(BF16)
    kxb = kx_ref[0].astype(BF16)
    vxb = vx_ref[0].astype(BF16)
    qpos = n * tq + _iota((tq, 3 * tq), 0)
    kpos = (n - 1) * tq + _iota((tq, 3 * tq), 1)
    valid = (jnp.abs(qpos - kpos) <= WINDOW) & (kpos >= 0) & (kpos < seq)
    lower = _iota((tq, LANE), 1) < HD
    scale = HD ** -0.5
    for c in range(QW // LANE):
        qc = q_ref[:, c * LANE:(c + 1) * LANE]
        halves = []
        for e in range(N_KV):
            qe = jnp.where(lower if e == 0 else jnp.logical_not(lower), qc, 0.0).astype(BF16)
            s = jnp.where(valid, _dot_nt(qe, kcat) * scale, -jnp.inf)
            sx = _dot_nt(qe, kxb) * scale
            sk = sink_ref[c * N_KV + e]
            m = jnp.maximum(jnp.maximum(jnp.max(s, axis=-1, keepdims=True),
                                        jnp.max(sx, axis=-1, keepdims=True)), sk)
            p = jnp.exp(s - m)
            px = jnp.exp(sx - m)
            l = jnp.sum(p, axis=-1, keepdims=True) + jnp.sum(px, axis=-1, keepdims=True) + jnp.exp(sk - m)
            halves.append((_dot(p, vcat) + _dot(px, vxb)) / l)
        oc = jnp.where(lower, halves[0], halves[1])
        o_ref[:, c * LANE:(c + 1) * LANE] = oc * _silu(g_ref[:, c * LANE:(c + 1) * LANE])


def _attn_band(z, ctx, sink, *, n_batch, seq):
    T = z.shape[0]
    nq = seq // Q_BLOCK
    n_ctx = ctx[0].shape[1]

    def kv_spec(off, delta):
        return pl.BlockSpec(
            (Q_BLOCK, KVW),
            lambda b, i: (b * nq + jnp.clip(i + delta, 0, nq - 1), off // KVW))

    specs = [
        pl.BlockSpec((Q_BLOCK, QW), lambda b, i: (b * nq + i, O_Q // QW)),
        kv_spec(O_K, -1), kv_spec(O_K, 0), kv_spec(O_K, 1),
        kv_spec(O_V, -1), kv_spec(O_V, 0), kv_spec(O_V, 1),
        pl.BlockSpec((Q_BLOCK, QW), lambda b, i: (b * nq + i, O_G // QW)),
        pl.BlockSpec((1, n_ctx, KVW), lambda b, i: (b, 0, 0)),
        pl.BlockSpec((1, n_ctx, KVW), lambda b, i: (b, 0, 0)),
        pl.BlockSpec(memory_space=pltpu.SMEM),
    ]
    return pl.pallas_call(
        functools.partial(_attn_band_kernel, seq=seq),
        grid=(n_batch, nq),
        in_specs=specs,
        out_specs=pl.BlockSpec((Q_BLOCK, QW), lambda b, i: (b * nq + i, 0)),
        out_shape=jax.ShapeDtypeStruct((T, QW), F32),
        compiler_params=_cparams(("parallel", "parallel")),
        name="attn_band",
    )(z, z, z, z, z, z, z, z, ctx[0], ctx[1], sink)


def _gla_chunk(q, k, v, r, up, bias, st_ref, rev):
    ii = _iota((CHUNK, CHUNK), 0)
    jj = _iota((CHUNK, CHUNK), 1)
    incl = (jj >= ii) if rev else (jj <= ii)
    tri = jnp.where(incl, 1.0, 0.0).astype(BF16)
    gk = _dot(r, up) + bias
    g = -_softplus(-gk) * (1.0 / GLA_NORMALIZER)
    b = _dot_exact_lhs(tri, g)
    btot = b[0:1, :] if rev else b[CHUNK - 1:CHUNK, :]
    qt = q * (B_DK ** -0.5) * jnp.exp(b)
    kt = k * jnp.exp(-b)
    kend = k * jnp.exp(btot - b)
    dec = jnp.exp(_dot_tn_exact_rhs(g, jnp.ones((CHUNK, B_DV), BF16)))
    st = st_ref[...]
    lower = _iota((CHUNK, LANE), 1) < B_DK
    outs, upd = [], []
    for c in range(B_HEADS * B_DK // LANE):
        qtc = qt[:, c * LANE:(c + 1) * LANE]
        ktc = kt[:, c * LANE:(c + 1) * LANE]
        kendc = kend[:, c * LANE:(c + 1) * LANE]
        stc = st[c * LANE:(c + 1) * LANE]
        for e in range(LANE // B_DK):
            h = c * (LANE // B_DK) + e
            qm = jnp.where(lower if e == 0 else jnp.logical_not(lower), qtc, 0.0)
            att = jnp.where(incl, _dot_nt(qm, ktc), 0.0)
            vh = v[:, h * B_DV:(h + 1) * B_DV]
            outs.append(_dot(att, vh) + _dot(qm, stc))
            upd.append(_dot_tn(kendc, vh)[e * B_DK:(e + 1) * B_DK])
    st_ref[...] = dec * st + jnp.concatenate(upd, axis=0)
    return jnp.concatenate(outs, axis=1)


def _gla_kernel(*refs, has_s0, n_chunk):
    qf, kf, vf, rf, qb, kb, vb, rb, up_ref, bias_ref = refs[:10]
    pos = 10
    if has_s0:
        s0f, s0b = refs[pos:pos + 2]
        pos += 2
    of_ref, ob_ref, sf_ref, sb_ref, stf, stb = refs[pos:pos + 6]
    n = pl.program_id(1)

    @pl.when(n == 0)
    def _():
        if has_s0:
            stf[...] = s0f[0]
            stb[...] = s0b[0]
        else:
            stf[...] = jnp.zeros_like(stf)
            stb[...] = jnp.zeros_like(stb)

    def body(ci, carry):
        rf_rows = pl.ds(pl.multiple_of(ci * CHUNK, CHUNK), CHUNK)
        of_ref[rf_rows, :] = _gla_chunk(qf[rf_rows, :], kf[rf_rows, :], vf[rf_rows, :], rf[rf_rows, :],
                                        up_ref[0], bias_ref[0], stf, False)
        rb_rows = pl.ds(pl.multiple_of((n_chunk - 1 - ci) * CHUNK, CHUNK), CHUNK)
        ob_ref[rb_rows, :] = _gla_chunk(qb[rb_rows, :], kb[rb_rows, :], vb[rb_rows, :], rb[rb_rows, :],
                                        up_ref[1], bias_ref[1], stb, True)
        return carry

    lax.fori_loop(0, n_chunk, body, 0)

    @pl.when(n == pl.num_programs(1) - 1)
    def _():
        sf_ref[0] = stf[...]
        sb_ref[0] = stb[...]


def _gla(z, up_ext, bias, s0, *, n_batch, seq):
    T = z.shape[0]
    nt = seq // REC_TILE
    has_s0 = s0 is not None
    sw = B_HEADS * B_DK

    def fwd(b, i):
        return b * nt + i

    def bwd(b, i):
        return b * nt + nt - 1 - i

    def specs_for(row):
        return [
            pl.BlockSpec((REC_TILE, sw), lambda b, i: (row(b, i), E_BQ // sw)),
            pl.BlockSpec((REC_TILE, sw), lambda b, i: (row(b, i), E_BK // sw)),
            pl.BlockSpec((REC_TILE, QW), lambda b, i: (row(b, i), E_BV // QW)),
            pl.BlockSpec((REC_TILE, LANE), lambda b, i: (row(b, i), E_R // LANE)),
        ]

    ins = [z] * 8 + [up_ext, bias]
    specs = specs_for(fwd) + specs_for(bwd) + [
        pl.BlockSpec((2, LANE, sw), lambda b, i: (0, 0, 0)),
        pl.BlockSpec((2, 1, sw), lambda b, i: (0, 0, 0)),
    ]
    if has_s0:
        ins += [s0[0], s0[1]]
        specs += [pl.BlockSpec((1, sw, B_DV), lambda b, i: (b, 0, 0))] * 2
    st_shape = jax.ShapeDtypeStruct((n_batch, sw, B_DV), F32)
    o_shape = jax.ShapeDtypeStruct((T, QW), F32)
    return pl.pallas_call(
        functools.partial(_gla_kernel, has_s0=has_s0, n_chunk=REC_TILE // CHUNK),
        grid=(n_batch, nt),
        in_specs=specs,
        out_specs=[
            pl.BlockSpec((REC_TILE, QW), lambda b, i: (fwd(b, i), 0)),
            pl.BlockSpec((REC_TILE, QW), lambda b, i: (bwd(b, i), 0)),
            pl.BlockSpec((1, sw, B_DV), lambda b, i: (b, 0, 0)),
            pl.BlockSpec((1, sw, B_DV), lambda b, i: (b, 0, 0)),
        ],
        out_shape=[o_shape, o_shape, st_shape, st_shape],
        scratch_shapes=[pltpu.VMEM((sw, B_DV), F32), pltpu.VMEM((sw, B_DV), F32)],
        compiler_params=_cparams(("arbitrary", "arbitrary")),
        name="gla",
    )(*ins)


def _delta_prep_kernel(cur_ref, prev_ref, next_ref, w_ref, o_ref, xe_ref, *, tiles_per_batch):
    i = pl.program_id(0)
    first = (i % tiles_per_batch) == 0
    last = (i % tiles_per_batch) == tiles_per_batch - 1
    xe_ref[0:HALO, :] = jnp.where(first, 0.0, prev_ref[...])
    xe_ref[HALO:HALO + REC_TILE, :] = cur_ref[...]
    xe_ref[HALO + REC_TILE:, :] = jnp.where(last, 0.0, next_ref[...])
    acc = None
    for j in range(CONV_K):
        start = HALO - CONV_K // 2 + j
        term = xe_ref[start:start + REC_TILE, :] * w_ref[j:j + 1, :]
        acc = term if acc is None else acc + term
    y = _silu(acc)
    qk_w = 2 * D_HEADS * D_DK
    for h in range(2 * D_HEADS):
        seg = y[:, h * D_DK:(h + 1) * D_DK]
        nrm = seg * lax.rsqrt(jnp.sum(seg * seg, axis=-1, keepdims=True) + EPS)
        if h < D_HEADS:
            nrm = nrm * (D_DK ** -0.5)
        o_ref[:, h * D_DK:(h + 1) * D_DK] = nrm
    o_ref[:, qk_w:] = y[:, qk_w:]


def _delta_prep(z, conv_w, *, seq):
    T = z.shape[0]
    tpb = seq // REC_TILE
    cw = 3 * D_HEADS * D_DK
    hb = REC_TILE // HALO
    n_hb = T // HALO
    return pl.pallas_call(
        functools.partial(_delta_prep_kernel, tiles_per_batch=tpb),
        grid=(T // REC_TILE,),
        in_specs=[
            pl.BlockSpec((REC_TILE, cw), lambda i: (i, 0)),
            pl.BlockSpec((HALO, cw), lambda i: (jnp.maximum(i * hb - 1, 0), 0)),
            pl.BlockSpec((HALO, cw), lambda i: (jnp.minimum((i + 1) * hb, n_hb - 1), 0)),
            pl.BlockSpec((CONV_K, cw), lambda i: (0, 0)),
        ],
        out_specs=pl.BlockSpec((REC_TILE, cw), lambda i: (i, 0)),
        out_shape=jax.ShapeDtypeStruct((T, cw), F32),
        scratch_shapes=[pltpu.VMEM((REC_TILE + 2 * HALO, cw), F32)],
        compiler_params=_cparams(("parallel",)),
        name="delta_prep",
    )(z, z, z, conv_w)


def _unit_lower_inverse(t):
    eye = jnp.where(_iota((CHUNK, CHUNK), 0) == _iota((CHUNK, CHUNK), 1), 1.0, 0.0)
    p = -t
    acc = eye + p
    m = 1
    while 2 * m < CHUNK + 1:
        p = _dot3(p, p)
        acc = acc + _dot3(acc, p)
        m *= 2
    return acc


def _delta_chunk(qkv, ab, alog, dtb, st_ref, d, rev):
    ii = _iota((CHUNK, CHUNK), 0)
    jj = _iota((CHUNK, CHUNK), 1)
    incl = (jj >= ii) if rev else (jj <= ii)
    strict = (jj > ii) if rev else (jj < ii)
    incl_t = (jj <= ii) if rev else (jj >= ii)
    tri = jnp.where(incl, 1.0, 0.0).astype(BF16)
    tri_t = jnp.where(incl_t, 1.0, 0.0).astype(BF16)
    gall = -jnp.exp(alog) * _softplus(ab + dtb)
    beta_all = _sigmoid(ab)
    gc_all = _dot_exact_lhs(tri, gall)
    gct_all = _dot_tn_exact_rhs(gall, tri_t)
    hk = D_HEADS * D_DK
    outs = []
    for h in range(D_HEADS):
        ig = d * D_HEADS + h
        ib = 2 * D_HEADS + ig
        gc = gc_all[:, ig:ig + 1]
        gr = gct_all[ig:ig + 1, :]
        beta = beta_all[:, ib:ib + 1]
        decay = jnp.exp(jnp.where(incl, gc - gr, -jnp.inf))
        glast = gc[0:1, :] if rev else gc[CHUNK - 1:CHUNK, :]
        q = qkv[:, h * D_DK:(h + 1) * D_DK]
        k = qkv[:, hk + h * D_DK:hk + (h + 1) * D_DK]
        v = qkv[:, 2 * hk + h * D_DV:2 * hk + (h + 1) * D_DV]
        kb = k * beta
        t = jnp.where(strict, _dot_nt(kb, k) * decay, 0.0)
        ainv = _unit_lower_inverse(t)
        egc = jnp.exp(gc)
        sol = _dot3(ainv, jnp.concatenate([v * beta, kb * egc], axis=1))
        u = sol[:, :D_DV]
        w = sol[:, D_DV:]
        att = _dot_nt(q, k) * decay
        s = st_ref[h]
        vnew = u - _dot(w, s)
        outs.append(_dot(q * egc, s) + _dot(att, vnew))
        st_ref[h] = s * jnp.exp(glast) + _dot_tn(k * jnp.exp(glast - gc), vnew)
    return jnp.concatenate(outs, axis=1)


def _delta_kernel(*refs, has_s0, n_chunk):
    xf, af, xb, ab_, alog_ref, dtb_ref = refs[:6]
    pos = 6
    if has_s0:
        s0f, s0b = refs[pos:pos + 2]
        pos += 2
    of_ref, ob_ref, sf_ref, sb_ref, stf, stb = refs[pos:pos + 6]
    n = pl.program_id(1)

    @pl.when(n == 0)
    def _():
        if has_s0:
            stf[...] = s0f[0]
            stb[...] = s0b[0]
        else:
            stf[...] = jnp.zeros_like(stf)
            stb[...] = jnp.zeros_like(stb)

    alog = alog_ref[...]
    dtb = dtb_ref[...]

    def body(ci, carry):
        rf_rows = pl.ds(pl.multiple_of(ci * CHUNK, CHUNK), CHUNK)
        of_ref[rf_rows, :] = _delta_chunk(xf[rf_rows, :], af[rf_rows, :], alog, dtb, stf, 0, False)
        rb_rows = pl.ds(pl.multiple_of((n_chunk - 1 - ci) * CHUNK, CHUNK), CHUNK)
        ob_ref[rb_rows, :] = _delta_chunk(xb[rb_rows, :], ab_[rb_rows, :], alog, dtb, stb, 1, True)
        return carry

    lax.fori_loop(0, n_chunk, body, 0)

    @pl.when(n == pl.num_programs(1) - 1)
    def _():
        sf_ref[0] = stf[...]
        sb_ref[0] = stb[...]


def _delta(qkv, z, alog_row, dtb_row, s0, *, n_batch, seq):
    T = z.shape[0]
    nt = seq // REC_TILE
    has_s0 = s0 is not None
    cw = 3 * D_HEADS * D_DK

    def fwd(b, i):
        return b * nt + i

    def bwd(b, i):
        return b * nt + nt - 1 - i

    ins = [qkv, z, qkv, z, alog_row, dtb_row]
    specs = [
        pl.BlockSpec((REC_TILE, cw), lambda b, i: (fwd(b, i), 0)),
        pl.BlockSpec((REC_TILE, LANE), lambda b, i: (fwd(b, i), O_AB // LANE)),
        pl.BlockSpec((REC_TILE, cw), lambda b, i: (bwd(b, i), 0)),
        pl.BlockSpec((REC_TILE, LANE), lambda b, i: (bwd(b, i), O_AB // LANE)),
        pl.BlockSpec((1, LANE), lambda b, i: (0, 0)),
        pl.BlockSpec((1, LANE), lambda b, i: (0, 0)),
    ]
    st_block = (1, D_HEADS, D_DK, D_DV)
    if has_s0:
        ins += [s0[0], s0[1]]
        specs += [pl.BlockSpec(st_block, lambda b, i: (b, 0, 0, 0))] * 2
    st_shape = jax.ShapeDtypeStruct((n_batch, D_HEADS, D_DK, D_DV), F32)
    o_shape = jax.ShapeDtypeStruct((T, QW), F32)
    return pl.pallas_call(
        functools.partial(_delta_kernel, has_s0=has_s0, n_chunk=REC_TILE // CHUNK),
        grid=(n_batch, nt),
        in_specs=specs,
        out_specs=[
            pl.BlockSpec((REC_TILE, QW), lambda b, i: (fwd(b, i), 0)),
            pl.BlockSpec((REC_TILE, QW), lambda b, i: (bwd(b, i), 0)),
            pl.BlockSpec(st_block, lambda b, i: (b, 0, 0, 0)),
            pl.BlockSpec(st_block, lambda b, i: (b, 0, 0, 0)),
        ],
        out_shape=[o_shape, o_shape, st_shape, st_shape],
        scratch_shapes=[pltpu.VMEM((D_HEADS, D_DK, D_DV), F32), pltpu.VMEM((D_HEADS, D_DK, D_DV), F32)],
        compiler_params=_cparams(("arbitrary", "arbitrary")),
        name="delta",
    )(*ins)


def _outproj_kernel(ya_ref, of_ref, ob_ref, zg_ref, on_ref, w_ref, x_ref, gate_ref, gp_ref, o_ref):
    ob = of_ref[...] + ob_ref[...]
    zg = zg_ref[...]
    on = on_ref[...]
    parts = []
    for h in range(QW // LANE):
        seg = ob[:, h * LANE:(h + 1) * LANE]
        nrm = seg * lax.rsqrt(jnp.mean(seg * seg, axis=-1, keepdims=True) + EPS) * on
        parts.append(nrm * _silu(zg[:, h * LANE:(h + 1) * LANE]))
    yb = jnp.concatenate(parts, axis=1)
    out = _dot(ya_ref[...], w_ref[0:QW, :]) + _dot(yb, w_ref[QW:, :])
    post = out * lax.rsqrt(jnp.mean(out * out, axis=-1, keepdims=True) + EPS) * gp_ref[...]
    o_ref[...] = x_ref[...] + gate_ref[0] * post


def _outproj(ya, o_f, o_b, z, g_off, onorm, w_out, x, gate, g_post, *, rows_per_batch):
    T = x.shape[0]
    tpb = rows_per_batch // ROW_TILE
    row = lambda i: (i, 0)
    per_batch = gate.shape[0] > 1
    mod_idx = (lambda i: (i // tpb, 0, 0)) if per_batch else (lambda i: (0, 0, 0))
    return pl.pallas_call(
        _outproj_kernel,
        grid=(T // ROW_TILE,),
        in_specs=[
            pl.BlockSpec((ROW_TILE, QW), row),
            pl.BlockSpec((ROW_TILE, QW), row),
            pl.BlockSpec((ROW_TILE, QW), row),
            pl.BlockSpec((ROW_TILE, QW), lambda i: (i, g_off // QW)),
            pl.BlockSpec((1, LANE), lambda i: (0, 0)),
            pl.BlockSpec((2 * QW, D_MODEL), lambda i: (0, 0)),
            pl.BlockSpec((ROW_TILE, D_MODEL), row),
            pl.BlockSpec((1, 1, D_MODEL), mod_idx),
            pl.BlockSpec((1, D_MODEL), lambda i: (0, 0)),
        ],
        out_specs=pl.BlockSpec((ROW_TILE, D_MODEL), row),
        out_shape=jax.ShapeDtypeStruct((T, D_MODEL), F32),
        compiler_params=_cparams(("parallel",)),
        name="outproj",
    )(ya, o_f, o_b, z, onorm.reshape(1, LANE), w_out, x, gate, g_post.reshape(1, D_MODEL))


def _head_perm():
    g = N_QH // N_KV
    return np.array([(e * g + c) * HD + d for c in range(g) for e in range(N_KV) for d in range(HD)])


def _rope_tables(seq):
    n_rows = seq // GRID_W
    rows = jnp.repeat(jnp.arange(n_rows, dtype=F32), GRID_W)
    cols = jnp.tile(jnp.arange(GRID_W, dtype=F32), n_rows)
    inv = jnp.power(ROPE_THETA, jnp.arange(ROT_AXIS // 2, dtype=F32) * (-2.0 / ROT_AXIS))
    ang_r = rows[:, None] * inv[None, :]
    ang_c = cols[:, None] * inv[None, :]
    zero = jnp.zeros_like(ang_r)
    cos = jnp.concatenate([jnp.cos(ang_r)] * 2 + [jnp.cos(ang_c)] * 2, axis=1)
    sa = jnp.concatenate([-jnp.sin(ang_r), zero, -jnp.sin(ang_c), zero], axis=1)
    sb = jnp.concatenate([zero, jnp.sin(ang_r), zero, jnp.sin(ang_c)], axis=1)
    return tuple(jnp.tile(t, (1, LANE // HD)) for t in (cos, sa, sb))


def _prep_even_weights(w_in, w_out, gk_up):
    perm = _head_perm()
    a_q, a_k, a_v, a_g, b_q, b_k, b_v, b_r, b_g = jnp.split(
        w_in, [512, 640, 768, 1280, 1536, 1792, 2304, 2336], axis=1)
    pad = jnp.zeros((D_MODEL, LANE - 2 * B_RANK), F32)
    w = jnp.concatenate([a_q[:, perm], a_g[:, perm], b_v, b_g, b_q, b_k, a_k, a_v, b_r, pad], axis=1)
    wo = jnp.concatenate([w_out[:QW][perm], w_out[QW:]], axis=0)
    up = jnp.zeros((2, LANE, B_HEADS * B_DK), F32)
    up = up.at[0, 0:B_RANK].set(gk_up[0]).at[1, B_RANK:2 * B_RANK].set(gk_up[1])
    return w.astype(BF16), wo.astype(BF16), up.astype(BF16)


def _prep_odd_weights(w_in, w_out):
    perm = _head_perm()
    c_q, c_k, c_v, c_g, d_q, d_k, d_v, d_a, d_b, d_g = jnp.split(
        w_in, [512, 640, 768, 1280, 1792, 2304, 2816, 2824, 2832], axis=1)
    pad = jnp.zeros((D_MODEL, LANE - 4 * D_HEADS), F32)
    w = jnp.concatenate([d_q, d_k, d_v, c_q[:, perm], c_g[:, perm], d_g, c_k, c_v, d_a, d_b, pad], axis=1)
    wo = jnp.concatenate([w_out[:QW][perm], w_out[QW:]], axis=0)
    return w.astype(BF16), wo.astype(BF16)


def _lane_row(x):
    flat = x.reshape(-1).astype(F32)
    return jnp.zeros((1, LANE), F32).at[0, :flat.shape[0]].set(flat)


def kernel(x_prompt, x_sample, c, cache_attn_k, cache_attn_v, state_gla, cache_swa_k, cache_swa_v, state_delta,
           c_ctx, w_mod, b_mod, g_pre, g_post, w_in_even, w_out_even, qnorm_a, knorm_a, gla_gk_up, gla_gk_bias,
           gla_onorm, w_in_odd, w_out_odd, sink_c, conv_d, a_log_d, dt_bias_d, delta_onorm):
    n_p, seq_p, _ = x_prompt.shape
    n_s, seq_s, _ = x_sample.shape
    n_ctx = cache_attn_k.shape[2]
    assert seq_p % ROW_TILE == 0 and seq_s % ROW_TILE == 0 and seq_p % REC_TILE == 0 and seq_s % REC_TILE == 0
    assert n_s + 1 <= 8

    cond = jnp.zeros((8, D_MODEL), F32).at[0].set(c_ctx).at[1:1 + n_s].set(c)
    mod = _modulation(cond, w_mod, b_mod)
    rope = _rope_tables(seq_s)
    sink_perm = np.array([e * (N_QH // N_KV) + cc for cc in range(N_QH // N_KV) for e in range(N_KV)])
    bd = jnp.asarray(np.kron(np.eye(LANE // HD), np.ones((HD, HD))), BF16)

    y_p = x_prompt.reshape(n_p * seq_p, D_MODEL)
    y_s = x_sample.reshape(n_s * seq_s, D_MODEL)
    attn_k, attn_v, gla_s, swa_k, swa_v, delta_s = [], [], [], [], [], []
    for l in range(DEPTH):
        i = l // 2
        shift, scale, gate = [mod[l, :, j * D_MODEL:(j + 1) * D_MODEL].reshape(8, 1, D_MODEL) for j in range(3)]
        mods_p = (shift[0:1], scale[0:1], gate[0:1])
        mods_s = (shift[1:1 + n_s], scale[1:1 + n_s], gate[1:1 + n_s])
        if l % 2 == 0:
            w, wo, up = _prep_even_weights(w_in_even[i], w_out_even[i], gla_gk_up[i])
            qn = jnp.tile(qnorm_a[i], LANE // HD).reshape(1, LANE)
            kn = jnp.tile(knorm_a[i], LANE // HD).reshape(1, LANE)
            bias = gla_gk_bias[i].reshape(2, 1, B_HEADS * B_DK)
            common = dict(even=True, qn=qn, kn=kn, bd=bd)
            z_p = _inproj(y_p, mods_p[0], mods_p[1], g_pre[l], w, rows_per_batch=seq_p, **common)
            z_s = _inproj(y_s, mods_s[0], mods_s[1], g_pre[l], w, rows_per_batch=seq_s, rope_tabs=rope, **common)
            offs = dict(q_off=E_Q, g_off=E_G, k_off=E_K, v_off=E_V)
            ya_p = _attn_dense(z_p, n_batch=n_p, seq=seq_p, tq=seq_p, **offs)
            ctx = (cache_attn_k[:, i].reshape(n_s, n_ctx, KVW), cache_attn_v[:, i].reshape(n_s, n_ctx, KVW))
            ya_s = _attn_dense(z_s, n_batch=n_s, seq=seq_s, tq=256, ctx=ctx, **offs)
            of_p, ob_p, sf_p, sb_p = _gla(z_p, up, bias, None, n_batch=n_p, seq=seq_p)
            s0 = state_gla[:, i].reshape(n_s, 2, B_HEADS * B_DK, B_DV)
            of_s, ob_s, _, _ = _gla(z_s, up, bias, (s0[:, 0], s0[:, 1]), n_batch=n_s, seq=seq_s)
            attn_k.append(z_p[:, E_K:E_K + KVW].reshape(n_p, seq_p, N_KV, HD))
            attn_v.append(z_p[:, E_V:E_V + KVW].reshape(n_p, seq_p, N_KV, HD))
            gla_s.append(jnp.stack([sf_p, sb_p], axis=1).reshape(n_p, 2, B_HEADS, B_DK, B_DV))
            g_off, onorm = E_BG, gla_onorm[i]
        else:
            w, wo = _prep_odd_weights(w_in_odd[i], w_out_odd[i])
            z_p = _inproj(y_p, mods_p[0], mods_p[1], g_pre[l], w, even=False, rows_per_batch=seq_p)
            z_s = _inproj(y_s, mods_s[0], mods_s[1], g_pre[l], w, even=False, rows_per_batch=seq_s,
                          rope_tabs=rope)
            sink = sink_c[i][sink_perm].astype(F32)
            ya_p = _attn_dense(z_p, n_batch=n_p, seq=seq_p, tq=seq_p, q_off=O_Q, g_off=O_G, k_off=O_K,
                               v_off=O_V, sink=sink)
            ctx = (cache_swa_k[:, i].reshape(n_s, n_ctx, KVW), cache_swa_v[:, i].reshape(n_s, n_ctx, KVW))
            ya_s = _attn_band(z_s, ctx, sink, n_batch=n_s, seq=seq_s)
            alog_row = _lane_row(a_log_d[i])
            dtb_row = _lane_row(dt_bias_d[i])
            qkv_p = _delta_prep(z_p, conv_d[i], seq=seq_p)
            qkv_s = _delta_prep(z_s, conv_d[i], seq=seq_s)
            of_p, ob_p, sf_p, sb_p = _delta(qkv_p, z_p, alog_row, dtb_row, None, n_batch=n_p, seq=seq_p)
            s0 = state_delta[:, i]
            of_s, ob_s, _, _ = _delta(qkv_s, z_s, alog_row, dtb_row, (s0[:, 0], s0[:, 1]), n_batch=n_s, seq=seq_s)
            swa_k.append(z_p[:, O_K:O_K + KVW].reshape(n_p, seq_p, N_KV, HD))
            swa_v.append(z_p[:, O_V:O_V + KVW].reshape(n_p, seq_p, N_KV, HD))
            delta_s.append(jnp.stack([sf_p, sb_p], axis=1))
            g_off, onorm = O_DG, delta_onorm[i]
        y_p = _outproj(ya_p, of_p, ob_p, z_p, g_off, onorm, wo, y_p, mods_p[2], g_post[l], rows_per_batch=seq_p)
        y_s = _outproj(ya_s, of_s, ob_s, z_s, g_off, onorm, wo, y_s, mods_s[2], g_post[l], rows_per_batch=seq_s)
    return (y_p.reshape(n_p, seq_p, D_MODEL), y_s.reshape(n_s, seq_s, D_MODEL),
            jnp.stack(attn_k, axis=1), jnp.stack(attn_v, axis=1), jnp.stack(gla_s, axis=1),
            jnp.stack(swa_k, axis=1), jnp.stack(swa_v, axis=1), jnp.stack(delta_s, axis=1))
```

```python
import functools

import numpy as np
import jax
import jax.numpy as jnp
from jax import lax
from jax.experimental import pallas as pl
from jax.experimental.pallas import tpu as pltpu

F32 = jnp.float32
BF16 = jnp.bfloat16

D_MODEL = 1024
DEPTH = 4
GRID_W = 64
HD = 64
N_QH = 8
N_KV = 2
QW = N_QH * HD
KVW = N_KV * HD
B_HEADS = 4
B_DK = 64
B_DV = 128
B_RANK = 16
GLA_NORMALIZER = 16.0
D_HEADS = 4
D_DK = 128
D_DV = 128
CONV_K = 5
WINDOW = 128
Q_BLOCK = 128
CHUNK = 64
ROPE_THETA = 10000.0
ROT_AXIS = HD // 2
EPS = 1e-6
LANE = 128
HALO = 8

ROW_TILE = 256
REC_TILE = 256
VMEM_LIMIT = 48 * 1024 * 1024

E_Q, E_G, E_BV, E_BG, E_BQ, E_BK, E_K, E_V, E_R, E_W = 0, 512, 1024, 1536, 2048, 2304, 2560, 2688, 2816, 2944
O_DQKV, O_Q, O_G, O_DG, O_K, O_V, O_AB, O_W = 0, 1536, 2048, 2560, 3072, 3200, 3328, 3456


def _sigmoid(x):
    return 1.0 / (1.0 + jnp.exp(-x))


def _silu(x):
    return x * _sigmoid(x)


def _softplus(x):
    return jnp.maximum(x, 0.0) + jnp.log(1.0 + jnp.exp(-jnp.abs(x)))


def _dot(a, b):
    return jnp.dot(a.astype(BF16), b.astype(BF16), preferred_element_type=F32)


def _dot_nt(a, b):
    return lax.dot_general(a.astype(BF16), b.astype(BF16), (((1,), (1,)), ((), ())),
                           preferred_element_type=F32)


def _dot_tn(a, b):
    return lax.dot_general(a.astype(BF16), b.astype(BF16), (((0,), (0,)), ((), ())),
                           preferred_element_type=F32)


def _split(x):
    hi = x.astype(BF16)
    lo = (x - hi.astype(F32)).astype(BF16)
    return hi, lo


def _dot_exact_lhs(m, x):
    hi, lo = _split(x)
    return _dot(m, hi) + _dot(m, lo)


def _dot_tn_exact_rhs(x, m):
    hi, lo = _split(x)
    return _dot_tn(hi, m) + _dot_tn(lo, m)


def _dot3(a, b):
    ah, al = _split(a)
    bh, bl = _split(b)
    return _dot(ah, bh) + (_dot(ah, bl) + _dot(al, bh))


def _iota(shape, dim):
    return lax.broadcasted_iota(jnp.int32, shape, dim)


def _cparams(sem):
    return pltpu.CompilerParams(dimension_semantics=sem, vmem_limit_bytes=VMEM_LIMIT)


def _mod_kernel(c_ref, w_ref, b_ref, o_ref):
    cs = _silu(c_ref[...])
    o_ref[0] = _dot(cs, w_ref[0]) + b_ref[0]


def _modulation(cond, w_mod, b_mod):
    tn = 1024
    return pl.pallas_call(
        _mod_kernel,
        grid=(DEPTH, 3 * D_MODEL // tn),
        in_specs=[
            pl.BlockSpec((8, D_MODEL), lambda l, j: (0, 0)),
            pl.BlockSpec((1, D_MODEL, tn), lambda l, j: (l, 0, j)),
            pl.BlockSpec((1, 1, tn), lambda l, j: (l, 0, j)),
        ],
        out_specs=pl.BlockSpec((1, 8, tn), lambda l, j: (l, 0, j)),
        out_shape=jax.ShapeDtypeStruct((DEPTH, 8, 3 * D_MODEL), F32),
        compiler_params=_cparams(("parallel", "parallel")),
        name="modulation",
    )(cond, w_mod, b_mod.reshape(DEPTH, 1, 3 * D_MODEL))


def _head_rms(x, bd, w):
    ss = _dot_exact_lhs_right(x * x, bd)
    return x * lax.rsqrt(ss * (1.0 / HD) + EPS) * w


def _dot_exact_lhs_right(x, m):
    hi, lo = _split(x)
    return _dot(hi, m) + _dot(lo, m)


def _rope(x, cos, sa, sb):
    return x * cos + pltpu.roll(x, LANE - ROT_AXIS // 2, axis=1) * sa + pltpu.roll(x, ROT_AXIS // 2, axis=1) * sb


def _inproj_kernel(*refs, even, rope):
    x_ref, shift_ref, scale_ref, g_ref, w_ref = refs[:5]
    pos = 5
    if even:
        qn_ref, kn_ref, bd_ref = refs[pos:pos + 3]
        pos += 3
    if rope:
        cos_ref, sa_ref, sb_ref = refs[pos:pos + 3]
        pos += 3
    o_ref = refs[pos]

    x = x_ref[...]
    y = x * lax.rsqrt(jnp.mean(x * x, axis=-1, keepdims=True) + EPS) * g_ref[...]
    h = (y * (1.0 + scale_ref[0]) + shift_ref[0]).astype(BF16)

    q_off, k_off = (E_Q, E_K) if even else (O_Q, O_K)
    width = E_W if even else O_W
    special = [q_off + c * LANE for c in range(QW // LANE)] + [k_off]

    def transform(z, is_q):
        if even:
            z = _head_rms(z, bd_ref[...], qn_ref[...] if is_q else kn_ref[...])
        if rope:
            z = _rope(z, cos_ref[...], sa_ref[...], sb_ref[...])
        return z

    if even or rope:
        for off in special:
            z = jnp.dot(h, w_ref[:, off:off + LANE], preferred_element_type=F32)
            o_ref[:, off:off + LANE] = transform(z, off != k_off)
        edges = sorted(special)
        spans, cur = [], 0
        for off in edges:
            if off > cur:
                spans.append((cur, off))
            cur = off + LANE
        if cur < width:
            spans.append((cur, width))
    else:
        spans = [(0, width)]
    for a, b in spans:
        step = 512
        for s in range(a, b, step):
            e = min(s + step, b)
            o_ref[:, s:e] = jnp.dot(h, w_ref[:, s:e], preferred_element_type=F32)


def _inproj(x, shift, scale, g_pre, w, *, even, rope_tabs=None, qn=None, kn=None, bd=None, rows_per_batch):
    T = x.shape[0]
    width = w.shape[1]
    tpb = rows_per_batch // ROW_TILE
    rope = rope_tabs is not None
    per_batch = shift.shape[0] > 1
    mod_idx = (lambda i: (i // tpb, 0, 0)) if per_batch else (lambda i: (0, 0, 0))
    ins = [x, shift, scale, g_pre.reshape(1, D_MODEL), w]
    specs = [
        pl.BlockSpec((ROW_TILE, D_MODEL), lambda i: (i, 0)),
        pl.BlockSpec((1, 1, D_MODEL), mod_idx),
        pl.BlockSpec((1, 1, D_MODEL), mod_idx),
        pl.BlockSpec((1, D_MODEL), lambda i: (0, 0)),
        pl.BlockSpec((D_MODEL, width), lambda i: (0, 0)),
    ]
    if even:
        ins += [qn, kn, bd]
        specs += [pl.BlockSpec((1, LANE), lambda i: (0, 0)),
                  pl.BlockSpec((1, LANE), lambda i: (0, 0)),
                  pl.BlockSpec((LANE, LANE), lambda i: (0, 0))]
    if rope:
        ins += list(rope_tabs)
        specs += [pl.BlockSpec((ROW_TILE, LANE), lambda i: (i % tpb, 0))] * 3
    return pl.pallas_call(
        functools.partial(_inproj_kernel, even=even, rope=rope),
        grid=(T // ROW_TILE,),
        in_specs=specs,
        out_specs=pl.BlockSpec((ROW_TILE, width), lambda i: (i, 0)),
        out_shape=jax.ShapeDtypeStruct((T, width), F32),
        compiler_params=_cparams(("parallel",)),
        name="inproj_even" if even else "inproj_odd",
    )(*ins)


def _attn_dense_kernel(*refs, has_ctx, has_sink, tq):
    q_ref, k_ref, v_ref, g_ref = refs[:4]
    pos = 4
    if has_ctx:
        kc_ref, vc_ref = refs[pos:pos + 2]
        pos += 2
    if has_sink:
        sink_ref = refs[pos]
        pos += 1
    o_ref, kb_ref, vb_ref = refs[pos:pos + 3]

    @pl.when(pl.program_id(1) == 0)
    def _():
        kb_ref[...] = k_ref[...].astype(BF16)
        vb_ref[...] = v_ref[...].astype(BF16)

    kb = kb_ref[...]
    vb = vb_ref[...]
    if has_ctx:
        kcb = kc_ref[0].astype(BF16)
        vcb = vc_ref[0].astype(BF16)
    lower = _iota((tq, LANE), 1) < HD
    scale = HD ** -0.5
    heads = [(c, e) for c in range(QW // LANE) for e in range(N_KV)]

    def scores(c, e):
        qc = q_ref[:, c * LANE:(c + 1) * LANE] * scale
        qe = jnp.where(lower if e == 0 else jnp.logical_not(lower), qc, 0.0).astype(BF16)
        return _dot_nt(qe, kb), (_dot_nt(qe, kcb) if has_ctx else None)

    def finish(c, e, s, sc):
        m = jnp.max(s, axis=-1, keepdims=True)
        if has_ctx:
            m = jnp.maximum(m, jnp.max(sc, axis=-1, keepdims=True))
        if has_sink:
            sk = sink_ref[c * N_KV + e]
            m = jnp.maximum(m, sk)
        p = jnp.exp(s - m)
        l = jnp.sum(p, axis=-1, keepdims=True)
        o = _dot(p, vb)
        if has_ctx:
            pc = jnp.exp(sc - m)
            l = l + jnp.sum(pc, axis=-1, keepdims=True)
            o = o + _dot(pc, vcb)
        if has_sink:
            l = l + jnp.exp(sk - m)
        return o / l

    halves = {}
    nxt = scores(*heads[0])
    for i, (c, e) in enumerate(heads):
        cur = nxt
        if i + 1 < len(heads):
            nxt = scores(*heads[i + 1])
        halves[e] = finish(c, e, *cur)
        if e == N_KV - 1:
            oc = jnp.where(lower, halves[0], halves[1])
            o_ref[:, c * LANE:(c + 1) * LANE] = oc * _silu(g_ref[:, c * LANE:(c + 1) * LANE])


def _attn_dense(z, *, n_batch, seq, tq, q_off, g_off, k_off, v_off, ctx=None, sink=None):
    T = z.shape[0]
    nq = seq // tq
    has_ctx = ctx is not None
    has_sink = sink is not None
    ins = [z, z, z, z]
    specs = [
        pl.BlockSpec((tq, QW), lambda b, i: (b * nq + i, q_off // QW)),
        pl.BlockSpec((seq, KVW), lambda b, i: (b, k_off // KVW)),
        pl.BlockSpec((seq, KVW), lambda b, i: (b, v_off // KVW)),
        pl.BlockSpec((tq, QW), lambda b, i: (b * nq + i, g_off // QW)),
    ]
    if has_ctx:
        ins += list(ctx)
        n_ctx = ctx[0].shape[1]
        specs += [pl.BlockSpec((1, n_ctx, KVW), lambda b, i: (b, 0, 0))] * 2
    if has_sink:
        ins.append(sink)
        specs.append(pl.BlockSpec(memory_space=pltpu.SMEM))
    return pl.pallas_call(
        functools.partial(_attn_dense_kernel, has_ctx=has_ctx, has_sink=has_sink, tq=tq),
        grid=(n_batch, nq),
        in_specs=specs,
        out_specs=pl.BlockSpec((tq, QW), lambda b, i: (b * nq + i, 0)),
        out_shape=jax.ShapeDtypeStruct((T, QW), F32),
        scratch_shapes=[pltpu.VMEM((seq, KVW), BF16), pltpu.VMEM((seq, KVW), BF16)],
        compiler_params=_cparams(("arbitrary", "arbitrary")),
        name="attn_dense",
    )(*ins)


def _attn_band_kernel(q_ref, kp_ref, kc_ref, kn_ref, vp_ref, vc_ref, vn_ref, g_ref, kx_ref, vx_ref, sink_ref,
                      o_ref, *, seq):
    n = pl.program_id(1)
    tq = Q_BLOCK
    kcat = jnp.concatenate([kp_ref[...], kc_ref[...], kn_ref[...]], axis=0).astype(BF16)
    vcat = jnp.concatenate([vp_ref[...], vc_ref[...], vn_ref[...]], axis=0).astype(BF16)
    kxb = kx_ref[0].astype(BF16)
    vxb = vx_ref[0].astype(BF16)
    qpos = n * tq + _iota((tq, 3 * tq), 0)
    kpos = (n - 1) * tq + _iota((tq, 3 * tq), 1)
    valid = (jnp.abs(qpos - kpos) <= WINDOW) & (kpos >= 0) & (kpos < seq)
    lower = _iota((tq, LANE), 1) < HD
    scale = HD ** -0.5
    heads = [(c, e) for c in range(QW // LANE) for e in range(N_KV)]

    def scores(c, e):
        qc = q_ref[:, c * LANE:(c + 1) * LANE] * scale
        qe = jnp.where(lower if e == 0 else jnp.logical_not(lower), qc, 0.0).astype(BF16)
        return _dot_nt(qe, kcat), _dot_nt(qe, kxb)

    def finish(c, e, s, sx):
        s = jnp.where(valid, s, -jnp.inf)
        sk = sink_ref[c * N_KV + e]
        m = jnp.maximum(jnp.maximum(jnp.max(s, axis=-1, keepdims=True),
                                    jnp.max(sx, axis=-1, keepdims=True)), sk)
        p = jnp.exp(s - m)
        px = jnp.exp(sx - m)
        l = jnp.sum(p, axis=-1, keepdims=True) + jnp.sum(px, axis=-1, keepdims=True) + jnp.exp(sk - m)
        return (_dot(p, vcat) + _dot(px, vxb)) / l

    halves = {}
    nxt = scores(*heads[0])
    for i, (c, e) in enumerate(heads):
        cur = nxt
        if i + 1 < len(heads):
            nxt = scores(*heads[i + 1])
        halves[e] = finish(c, e, *cur)
        if e == N_KV - 1:
            oc = jnp.where(lower, halves[0], halves[1])
            o_ref[:, c * LANE:(c + 1) * LANE] = oc * _silu(g_ref[:, c * LANE:(c + 1) * LANE])


def _attn_band(z, ctx, sink, *, n_batch, seq):
    T = z.shape[0]
    nq = seq // Q_BLOCK
    n_ctx = ctx[0].shape[1]

    def kv_spec(off, delta):
        return pl.BlockSpec(
            (Q_BLOCK, KVW),
            lambda b, i: (b * nq + jnp.clip(i + delta, 0, nq - 1), off // KVW))

    specs = [
        pl.BlockSpec((Q_BLOCK, QW), lambda b, i: (b * nq + i, O_Q // QW)),
        kv_spec(O_K, -1), kv_spec(O_K, 0), kv_spec(O_K, 1),
        kv_spec(O_V, -1), kv_spec(O_V, 0), kv_spec(O_V, 1),
        pl.BlockSpec((Q_BLOCK, QW), lambda b, i: (b * nq + i, O_G // QW)),
        pl.BlockSpec((1, n_ctx, KVW), lambda b, i: (b, 0, 0)),
        pl.BlockSpec((1, n_ctx, KVW), lambda b, i: (b, 0, 0)),
        pl.BlockSpec(memory_space=pltpu.SMEM),
    ]
    return pl.pallas_call(
        functools.partial(_attn_band_kernel, seq=seq),
        grid=(n_batch, nq),
        in_specs=specs,
        out_specs=pl.BlockSpec((Q_BLOCK, QW), lambda b, i: (b * nq + i, 0)),
        out_shape=jax.ShapeDtypeStruct((T, QW), F32),
        compiler_params=_cparams(("parallel", "parallel")),
        name="attn_band",
    )(z, z, z, z, z, z, z, z, ctx[0], ctx[1], sink)


def _gla_tile(dirs, n_chunk):
    rows = n_chunk * CHUNK
    shift = CHUNK.bit_length() - 1
    ii = _iota((rows, rows), 0)
    jj = _iota((rows, rows), 1)
    same = lax.shift_right_logical(ii, shift) == lax.shift_right_logical(jj, shift)
    sel = jnp.where(lax.shift_right_logical(_iota((rows, n_chunk * B_DV), 0), shift)
                    == lax.shift_right_logical(_iota((rows, n_chunk * B_DV), 1), B_DV.bit_length() - 1),
                    1.0, 0.0).astype(BF16)
    ci = _iota((CHUNK, CHUNK), 0)
    cj = _iota((CHUNK, CHUNK), 1)
    lower = _iota((CHUNK, LANE), 1) < B_DK
    n_pair = B_HEADS * B_DK // LANE
    per_pair = LANE // B_DK

    gs = []
    for (_, _, _, r_ref, up, bias, _, _, _) in dirs:
        gk = _dot(r_ref[...], up) + bias
        gs.append(-_softplus(-gk) * (1.0 / GLA_NORMALIZER))
    bs = []
    for g, d in zip(gs, dirs):
        tri = jnp.where(same & ((jj >= ii) if d[8] else (jj <= ii)), 1.0, 0.0).astype(BF16)
        bs.append(_dot_exact_lhs(tri, g))
    decargs = [_dot_tn_exact_rhs(g, sel) for g in gs]

    units = []
    for di, (q_ref, k_ref, v_ref, _, _, _, st_ref, o_ref, rev) in enumerate(dirs):
        b = bs[di]
        incl = (cj >= ci) if rev else (cj <= ci)
        order = range(n_chunk - 1, -1, -1) if rev else range(n_chunk)
        for c in order:
            r0 = c * CHUNK
            bc = b[r0:r0 + CHUNK]
            btot = bc[0:1, :] if rev else bc[CHUNK - 1:CHUNK, :]
            qt = q_ref[r0:r0 + CHUNK, :] * (B_DK ** -0.5) * jnp.exp(bc)
            kc = k_ref[r0:r0 + CHUNK, :]
            kt = kc * jnp.exp(-bc)
            kend = kc * jnp.exp(btot - bc)
            for p in range(n_pair):
                for e in range(per_pair):
                    h = p * per_pair + e
                    qm = jnp.where(lower if e == 0 else jnp.logical_not(lower), qt[:, p * LANE:(p + 1) * LANE], 0.0)
                    units.append(dict(di=di, c=c, p=p, e=e, incl=incl, qm=qm.astype(BF16),
                                      kt=kt[:, p * LANE:(p + 1) * LANE].astype(BF16),
                                      kend=kend[:, p * LANE:(p + 1) * LANE].astype(BF16),
                                      v=v_ref[r0:r0 + CHUNK, h * B_DV:(h + 1) * B_DV].astype(BF16)))
    for u in units:
        u["att"] = jnp.where(u["incl"], _dot_nt(u["qm"], u["kt"]), 0.0).astype(BF16)
    for u in units:
        u["upd"] = _dot_tn(u["kend"], u["v"])[u["e"] * B_DK:(u["e"] + 1) * B_DK]
    for di, d in enumerate(dirs):
        st = d[6][...]
        mine = [u for u in units if u["di"] == di]
        for k0 in range(0, len(mine), B_HEADS):
            grp = mine[k0:k0 + B_HEADS]
            c = grp[0]["c"]
            for u in grp:
                u["st"] = st[u["p"] * LANE:(u["p"] + 1) * LANE].astype(BF16)
            dec = jnp.exp(decargs[di][:, c * B_DV:(c + 1) * B_DV])
            st = dec * st + jnp.concatenate([u["upd"] for u in grp], axis=0)
        d[6][...] = st
    for u in units:
        u["qs"] = _dot(u["qm"], u["st"])
    for u in units:
        u["o"] = _dot(u["att"], u["v"]) + u["qs"]
    for di, d in enumerate(dirs):
        mine = [u for u in units if u["di"] == di]
        for k0 in range(0, len(mine), B_HEADS):
            grp = mine[k0:k0 + B_HEADS]
            r0 = grp[0]["c"] * CHUNK
            d[7][r0:r0 + CHUNK, :] = jnp.concatenate([u["o"] for u in grp], axis=1)


def _gla_kernel(*refs, has_s0, n_chunk):
    qf, kf, vf, rf, qb, kb, vb, rb, up_ref, bias_ref = refs[:10]
    pos = 10
    if has_s0:
        s0f, s0b = refs[pos:pos + 2]
        pos += 2
    of_ref, ob_ref, sf_ref, sb_ref, stf, stb = refs[pos:pos + 6]
    n = pl.program_id(1)

    @pl.when(n == 0)
    def _():
        if has_s0:
            stf[...] = s0f[0]
            stb[...] = s0b[0]
        else:
            stf[...] = jnp.zeros_like(stf)
            stb[...] = jnp.zeros_like(stb)

    _gla_tile([(qf, kf, vf, rf, up_ref[0], bias_ref[0], stf, of_ref, False),
               (qb, kb, vb, rb, up_ref[1], bias_ref[1], stb, ob_ref, True)], n_chunk)

    @pl.when(n == pl.num_programs(1) - 1)
    def _():
        sf_ref[0] = stf[...]
        sb_ref[0] = stb[...]


def _gla(z, up_ext, bias, s0, *, n_batch, seq):
    T = z.shape[0]
    nt = seq // REC_TILE
    has_s0 = s0 is not None
    sw = B_HEADS * B_DK

    def fwd(b, i):
        return b * nt + i

    def bwd(b, i):
        return b * nt + nt - 1 - i

    def specs_for(row):
        return [
            pl.BlockSpec((REC_TILE, sw), lambda b, i: (row(b, i), E_BQ // sw)),
            pl.BlockSpec((REC_TILE, sw), lambda b, i: (row(b, i), E_BK // sw)),
            pl.BlockSpec((REC_TILE, QW), lambda b, i: (row(b, i), E_BV // QW)),
            pl.BlockSpec((REC_TILE, LANE), lambda b, i: (row(b, i), E_R // LANE)),
        ]

    ins = [z] * 8 + [up_ext, bias]
    specs = specs_for(fwd) + specs_for(bwd) + [
        pl.BlockSpec((2, LANE, sw), lambda b, i: (0, 0, 0)),
        pl.BlockSpec((2, 1, sw), lambda b, i: (0, 0, 0)),
    ]
    if has_s0:
        ins += [s0[0], s0[1]]
        specs += [pl.BlockSpec((1, sw, B_DV), lambda b, i: (b, 0, 0))] * 2
    st_shape = jax.ShapeDtypeStruct((n_batch, sw, B_DV), F32)
    o_shape = jax.ShapeDtypeStruct((T, QW), F32)
    return pl.pallas_call(
        functools.partial(_gla_kernel, has_s0=has_s0, n_chunk=REC_TILE // CHUNK),
        grid=(n_batch, nt),
        in_specs=specs,
        out_specs=[
            pl.BlockSpec((REC_TILE, QW), lambda b, i: (fwd(b, i), 0)),
            pl.BlockSpec((REC_TILE, QW), lambda b, i: (bwd(b, i), 0)),
            pl.BlockSpec((1, sw, B_DV), lambda b, i: (b, 0, 0)),
            pl.BlockSpec((1, sw, B_DV), lambda b, i: (b, 0, 0)),
        ],
        out_shape=[o_shape, o_shape, st_shape, st_shape],
        scratch_shapes=[pltpu.VMEM((sw, B_DV), F32), pltpu.VMEM((sw, B_DV), F32)],
        compiler_params=_cparams(("arbitrary", "arbitrary")),
        name="gla",
    )(*ins)


def _delta_prep_kernel(cur_ref, prev_ref, next_ref, w_ref, o_ref, xe_ref, *, tiles_per_batch):
    i = pl.program_id(0)
    first = (i % tiles_per_batch) == 0
    last = (i % tiles_per_batch) == tiles_per_batch - 1
    xe_ref[0:HALO, :] = jnp.where(first, 0.0, prev_ref[...])
    xe_ref[HALO:HALO + REC_TILE, :] = cur_ref[...]
    xe_ref[HALO + REC_TILE:, :] = jnp.where(last, 0.0, next_ref[...])
    acc = None
    for j in range(CONV_K):
        start = HALO - CONV_K // 2 + j
        term = xe_ref[start:start + REC_TILE, :] * w_ref[j:j + 1, :]
        acc = term if acc is None else acc + term
    y = _silu(acc)
    qk_w = 2 * D_HEADS * D_DK
    for h in range(2 * D_HEADS):
        seg = y[:, h * D_DK:(h + 1) * D_DK]
        nrm = seg * lax.rsqrt(jnp.sum(seg * seg, axis=-1, keepdims=True) + EPS)
        if h < D_HEADS:
            nrm = nrm * (D_DK ** -0.5)
        o_ref[:, h * D_DK:(h + 1) * D_DK] = nrm
    o_ref[:, qk_w:] = y[:, qk_w:]


def _delta_prep(z, conv_w, *, seq):
    T = z.shape[0]
    tpb = seq // REC_TILE
    cw = 3 * D_HEADS * D_DK
    hb = REC_TILE // HALO
    n_hb = T // HALO
    return pl.pallas_call(
        functools.partial(_delta_prep_kernel, tiles_per_batch=tpb),
        grid=(T // REC_TILE,),
        in_specs=[
            pl.BlockSpec((REC_TILE, cw), lambda i: (i, 0)),
            pl.BlockSpec((HALO, cw), lambda i: (jnp.maximum(i * hb - 1, 0), 0)),
            pl.BlockSpec((HALO, cw), lambda i: (jnp.minimum((i + 1) * hb, n_hb - 1), 0)),
            pl.BlockSpec((CONV_K, cw), lambda i: (0, 0)),
        ],
        out_specs=pl.BlockSpec((REC_TILE, cw), lambda i: (i, 0)),
        out_shape=jax.ShapeDtypeStruct((T, cw), F32),
        scratch_shapes=[pltpu.VMEM((REC_TILE + 2 * HALO, cw), F32)],
        compiler_params=_cparams(("parallel",)),
        name="delta_prep",
    )(z, z, z, conv_w)


def _dot3_shared_rhs(lhs_list, rhs):
    rh, rl = _split(rhs)
    parts = [_split(x) for x in lhs_list]
    his = [p[0] for p in parts]
    los = [p[1] for p in parts]
    n = lhs_list[0].shape[0]
    k = len(lhs_list)
    r1 = _dot(jnp.concatenate(his + los, axis=0), rh)
    r2 = _dot(jnp.concatenate(his, axis=0) if k > 1 else his[0], rl)
    return [r1[i * n:(i + 1) * n] + (r1[(k + i) * n:(k + i + 1) * n] + r2[i * n:(i + 1) * n]) for i in range(k)]


def _delta_chunks(dirs, alog, dtb):
    ii = _iota((CHUNK, CHUNK), 0)
    jj = _iota((CHUNK, CHUNK), 1)
    eye = jnp.where(ii == jj, 1.0, 0.0)
    hk = D_HEADS * D_DK
    units = []
    for d, (qkv, ab, st_ref, rev) in enumerate(dirs):
        incl = (jj >= ii) if rev else (jj <= ii)
        strict = (jj > ii) if rev else (jj < ii)
        incl_t = (jj <= ii) if rev else (jj >= ii)
        tri = jnp.where(incl, 1.0, 0.0).astype(BF16)
        tri_t = jnp.where(incl_t, 1.0, 0.0).astype(BF16)
        gall = -jnp.exp(alog) * _softplus(ab + dtb)
        beta_all = _sigmoid(ab)
        gc_all = _dot_exact_lhs(tri, gall)
        gct_all = _dot_tn_exact_rhs(gall, tri_t)
        for h in range(D_HEADS):
            ig = d * D_HEADS + h
            ib = 2 * D_HEADS + ig
            gc = gc_all[:, ig:ig + 1]
            beta = beta_all[:, ib:ib + 1]
            k = qkv[:, hk + h * D_DK:hk + (h + 1) * D_DK]
            units.append(dict(
                d=d, h=h, st_ref=st_ref, strict=strict, gc=gc, beta=beta, k=k,
                decay=jnp.exp(jnp.where(incl, gc - gct_all[ig:ig + 1, :], -jnp.inf)),
                glast=gc[0:1, :] if rev else gc[CHUNK - 1:CHUNK, :],
                q=qkv[:, h * D_DK:(h + 1) * D_DK],
                v=qkv[:, 2 * hk + h * D_DV:2 * hk + (h + 1) * D_DV],
                kb=k * beta, egc=jnp.exp(gc)))
    for u in units:
        r = _dot_nt(jnp.concatenate([u["kb"], u["q"]], axis=0), u["k"])
        u["p"] = -jnp.where(u["strict"], r[:CHUNK] * u["decay"], 0.0)
        u["att"] = r[CHUNK:] * u["decay"]
        u["acc"] = eye + u["p"]
    for u in units:
        u["p"] = _dot3_shared_rhs([u["p"]], u["p"])[0]
    m = 2
    while m < CHUNK:
        last = 2 * m >= CHUNK
        for u in units:
            if last:
                u["acc"] = u["acc"] + _dot3_shared_rhs([u["acc"]], u["p"])[0]
            else:
                pp, ap = _dot3_shared_rhs([u["p"], u["acc"]], u["p"])
                u["acc"] = u["acc"] + ap
                u["p"] = pp
        m *= 2
    for u in units:
        rhs = jnp.concatenate([u["v"] * u["beta"], u["kb"] * u["egc"]], axis=1)
        u["sol"] = _dot3_shared_rhs([u["acc"]], rhs)[0]
    for u in units:
        u["s"] = u["st_ref"][u["h"]]
        r = _dot(jnp.concatenate([u["sol"][:, D_DV:], u["q"] * u["egc"]], axis=0), u["s"])
        u["vnew"] = u["sol"][:, :D_DV] - r[:CHUNK]
        u["qs"] = r[CHUNK:]
    for u in units:
        u["o"] = u["qs"] + _dot(u["att"], u["vnew"])
        u["st_ref"][u["h"]] = (u["s"] * jnp.exp(u["glast"])
                               + _dot_tn(u["k"] * jnp.exp(u["glast"] - u["gc"]), u["vnew"]))
    return [jnp.concatenate([u["o"] for u in units if u["d"] == d], axis=1) for d in range(len(dirs))]


def _delta_kernel(*refs, has_s0, n_chunk):
    xf, af, xb, ab_, alog_ref, dtb_ref = refs[:6]
    pos = 6
    if has_s0:
        s0f, s0b = refs[pos:pos + 2]
        pos += 2
    of_ref, ob_ref, sf_ref, sb_ref, stf, stb = refs[pos:pos + 6]
    n = pl.program_id(1)

    @pl.when(n == 0)
    def _():
        if has_s0:
            stf[...] = s0f[0]
            stb[...] = s0b[0]
        else:
            stf[...] = jnp.zeros_like(stf)
            stb[...] = jnp.zeros_like(stb)

    alog = alog_ref[...]
    dtb = dtb_ref[...]

    def body(ci, carry):
        rf_rows = pl.ds(pl.multiple_of(ci * CHUNK, CHUNK), CHUNK)
        rb_rows = pl.ds(pl.multiple_of((n_chunk - 1 - ci) * CHUNK, CHUNK), CHUNK)
        o_f, o_b = _delta_chunks([(xf[rf_rows, :], af[rf_rows, :], stf, False),
                                  (xb[rb_rows, :], ab_[rb_rows, :], stb, True)], alog, dtb)
        of_ref[rf_rows, :] = o_f
        ob_ref[rb_rows, :] = o_b
        return carry

    lax.fori_loop(0, n_chunk, body, 0)

    @pl.when(n == pl.num_programs(1) - 1)
    def _():
        sf_ref[0] = stf[...]
        sb_ref[0] = stb[...]


def _delta(qkv, z, alog_row, dtb_row, s0, *, n_batch, seq):
    T = z.shape[0]
    nt = seq // REC_TILE
    has_s0 = s0 is not None
    cw = 3 * D_HEADS * D_DK

    def fwd(b, i):
        return b * nt + i

    def bwd(b, i):
        return b * nt + nt - 1 - i

    ins = [qkv, z, qkv, z, alog_row, dtb_row]
    specs = [
        pl.BlockSpec((REC_TILE, cw), lambda b, i: (fwd(b, i), 0)),
        pl.BlockSpec((REC_TILE, LANE), lambda b, i: (fwd(b, i), O_AB // LANE)),
        pl.BlockSpec((REC_TILE, cw), lambda b, i: (bwd(b, i), 0)),
        pl.BlockSpec((REC_TILE, LANE), lambda b, i: (bwd(b, i), O_AB // LANE)),
        pl.BlockSpec((1, LANE), lambda b, i: (0, 0)),
        pl.BlockSpec((1, LANE), lambda b, i: (0, 0)),
    ]
    st_block = (1, D_HEADS, D_DK, D_DV)
    if has_s0:
        ins += [s0[0], s0[1]]
        specs += [pl.BlockSpec(st_block, lambda b, i: (b, 0, 0, 0))] * 2
    st_shape = jax.ShapeDtypeStruct((n_batch, D_HEADS, D_DK, D_DV), F32)
    o_shape = jax.ShapeDtypeStruct((T, QW), F32)
    return pl.pallas_call(
        functools.partial(_delta_kernel, has_s0=has_s0, n_chunk=REC_TILE // CHUNK),
        grid=(n_batch, nt),
        in_specs=specs,
        out_specs=[
            pl.BlockSpec((REC_TILE, QW), lambda b, i: (fwd(b, i), 0)),
            pl.BlockSpec((REC_TILE, QW), lambda b, i: (bwd(b, i), 0)),
            pl.BlockSpec(st_block, lambda b, i: (b, 0, 0, 0)),
            pl.BlockSpec(st_block, lambda b, i: (b, 0, 0, 0)),
        ],
        out_shape=[o_shape, o_shape, st_shape, st_shape],
        scratch_shapes=[pltpu.VMEM((D_HEADS, D_DK, D_DV), F32), pltpu.VMEM((D_HEADS, D_DK, D_DV), F32)],
        compiler_params=_cparams(("arbitrary", "arbitrary")),
        name="delta",
    )(*ins)


def _outproj_kernel(ya_ref, of_ref, ob_ref, zg_ref, on_ref, w_ref, x_ref, gate_ref, gp_ref, o_ref):
    ob = of_ref[...] + ob_ref[...]
    zg = zg_ref[...]
    on = on_ref[...]
    parts = []
    for h in range(QW // LANE):
        seg = ob[:, h * LANE:(h + 1) * LANE]
        nrm = seg * lax.rsqrt(jnp.mean(seg * seg, axis=-1, keepdims=True) + EPS) * on
        parts.append(nrm * _silu(zg[:, h * LANE:(h + 1) * LANE]))
    yb = jnp.concatenate(parts, axis=1)
    out = _dot(ya_ref[...], w_ref[0:QW, :]) + _dot(yb, w_ref[QW:, :])
    post = out * lax.rsqrt(jnp.mean(out * out, axis=-1, keepdims=True) + EPS) * gp_ref[...]
    o_ref[...] = x_ref[...] + gate_ref[0] * post


def _outproj(ya, o_f, o_b, z, g_off, onorm, w_out, x, gate, g_post, *, rows_per_batch):
    T = x.shape[0]
    tpb = rows_per_batch // ROW_TILE
    row = lambda i: (i, 0)
    per_batch = gate.shape[0] > 1
    mod_idx = (lambda i: (i // tpb, 0, 0)) if per_batch else (lambda i: (0, 0, 0))
    return pl.pallas_call(
        _outproj_kernel,
        grid=(T // ROW_TILE,),
        in_specs=[
            pl.BlockSpec((ROW_TILE, QW), row),
            pl.BlockSpec((ROW_TILE, QW), row),
            pl.BlockSpec((ROW_TILE, QW), row),
            pl.BlockSpec((ROW_TILE, QW), lambda i: (i, g_off // QW)),
            pl.BlockSpec((1, LANE), lambda i: (0, 0)),
            pl.BlockSpec((2 * QW, D_MODEL), lambda i: (0, 0)),
            pl.BlockSpec((ROW_TILE, D_MODEL), row),
            pl.BlockSpec((1, 1, D_MODEL), mod_idx),
            pl.BlockSpec((1, D_MODEL), lambda i: (0, 0)),
        ],
        out_specs=pl.BlockSpec((ROW_TILE, D_MODEL), row),
        out_shape=jax.ShapeDtypeStruct((T, D_MODEL), F32),
        compiler_params=_cparams(("parallel",)),
        name="outproj",
    )(ya, o_f, o_b, z, onorm.reshape(1, LANE), w_out, x, gate, g_post.reshape(1, D_MODEL))


def _head_perm():
    g = N_QH // N_KV
    return np.array([(e * g + c) * HD + d for c in range(g) for e in range(N_KV) for d in range(HD)])


def _rope_tables(seq):
    n_rows = seq // GRID_W
    rows = jnp.repeat(jnp.arange(n_rows, dtype=F32), GRID_W)
    cols = jnp.tile(jnp.arange(GRID_W, dtype=F32), n_rows)
    inv = jnp.power(ROPE_THETA, jnp.arange(ROT_AXIS // 2, dtype=F32) * (-2.0 / ROT_AXIS))
    ang_r = rows[:, None] * inv[None, :]
    ang_c = cols[:, None] * inv[None, :]
    zero = jnp.zeros_like(ang_r)
    cos = jnp.concatenate([jnp.cos(ang_r)] * 2 + [jnp.cos(ang_c)] * 2, axis=1)
    sa = jnp.concatenate([-jnp.sin(ang_r), zero, -jnp.sin(ang_c), zero], axis=1)
    sb = jnp.concatenate([zero, jnp.sin(ang_r), zero, jnp.sin(ang_c)], axis=1)
    return tuple(jnp.tile(t, (1, LANE // HD)) for t in (cos, sa, sb))


def _prep_even_weights(w_in, w_out, gk_up):
    perm = _head_perm()
    a_q, a_k, a_v, a_g, b_q, b_k, b_v, b_r, b_g = jnp.split(
        w_in, [512, 640, 768, 1280, 1536, 1792, 2304, 2336], axis=1)
    pad = jnp.zeros((D_MODEL, LANE - 2 * B_RANK), F32)
    w = jnp.concatenate([a_q[:, perm], a_g[:, perm], b_v, b_g, b_q, b_k, a_k, a_v, b_r, pad], axis=1)
    wo = jnp.concatenate([w_out[:QW][perm], w_out[QW:]], axis=0)
    up = jnp.zeros((2, LANE, B_HEADS * B_DK), F32)
    up = up.at[0, 0:B_RANK].set(gk_up[0]).at[1, B_RANK:2 * B_RANK].set(gk_up[1])
    return w.astype(BF16), wo.astype(BF16), up.astype(BF16)


def _prep_odd_weights(w_in, w_out):
    perm = _head_perm()
    c_q, c_k, c_v, c_g, d_q, d_k, d_v, d_a, d_b, d_g = jnp.split(
        w_in, [512, 640, 768, 1280, 1792, 2304, 2816, 2824, 2832], axis=1)
    pad = jnp.zeros((D_MODEL, LANE - 4 * D_HEADS), F32)
    w = jnp.concatenate([d_q, d_k, d_v, c_q[:, perm], c_g[:, perm], d_g, c_k, c_v, d_a, d_b, pad], axis=1)
    wo = jnp.concatenate([w_out[:QW][perm], w_out[QW:]], axis=0)
    return w.astype(BF16), wo.astype(BF16)


def _lane_row(x):
    flat = x.reshape(-1).astype(F32)
    return jnp.zeros((1, LANE), F32).at[0, :flat.shape[0]].set(flat)


def kernel(x_prompt, x_sample, c, cache_attn_k, cache_attn_v, state_gla, cache_swa_k, cache_swa_v, state_delta,
           c_ctx, w_mod, b_mod, g_pre, g_post, w_in_even, w_out_even, qnorm_a, knorm_a, gla_gk_up, gla_gk_bias,
           gla_onorm, w_in_odd, w_out_odd, sink_c, conv_d, a_log_d, dt_bias_d, delta_onorm):
    n_p, seq_p, _ = x_prompt.shape
    n_s, seq_s, _ = x_sample.shape
    n_ctx = cache_attn_k.shape[2]
    assert seq_p % ROW_TILE == 0 and seq_s % ROW_TILE == 0 and seq_p % REC_TILE == 0 and seq_s % REC_TILE == 0
    assert n_s + 1 <= 8

    cond = jnp.zeros((8, D_MODEL), F32).at[0].set(c_ctx).at[1:1 + n_s].set(c)
    mod = _modulation(cond, w_mod, b_mod)
    rope = _rope_tables(seq_s)
    sink_perm = np.array([e * (N_QH // N_KV) + cc for cc in range(N_QH // N_KV) for e in range(N_KV)])
    bd = jnp.asarray(np.kron(np.eye(LANE // HD), np.ones((HD, HD))), BF16)

    y_p = x_prompt.reshape(n_p * seq_p, D_MODEL)
    y_s = x_sample.reshape(n_s * seq_s, D_MODEL)
    attn_k, attn_v, gla_s, swa_k, swa_v, delta_s = [], [], [], [], [], []
    for l in range(DEPTH):
        i = l // 2
        shift, scale, gate = [mod[l, :, j * D_MODEL:(j + 1) * D_MODEL].reshape(8, 1, D_MODEL) for j in range(3)]
        mods_p = (shift[0:1], scale[0:1], gate[0:1])
        mods_s = (shift[1:1 + n_s], scale[1:1 + n_s], gate[1:1 + n_s])
        if l % 2 == 0:
            w, wo, up = _prep_even_weights(w_in_even[i], w_out_even[i], gla_gk_up[i])
            qn = jnp.tile(qnorm_a[i], LANE // HD).reshape(1, LANE)
            kn = jnp.tile(knorm_a[i], LANE // HD).reshape(1, LANE)
            bias = gla_gk_bias[i].reshape(2, 1, B_HEADS * B_DK)
            common = dict(even=True, qn=qn, kn=kn, bd=bd)
            z_p = _inproj(y_p, mods_p[0], mods_p[1], g_pre[l], w, rows_per_batch=seq_p, **common)
            z_s = _inproj(y_s, mods_s[0], mods_s[1], g_pre[l], w, rows_per_batch=seq_s, rope_tabs=rope, **common)
            offs = dict(q_off=E_Q, g_off=E_G, k_off=E_K, v_off=E_V)
            ya_p = _attn_dense(z_p, n_batch=n_p, seq=seq_p, tq=seq_p, **offs)
            ctx = (cache_attn_k[:, i].reshape(n_s, n_ctx, KVW), cache_attn_v[:, i].reshape(n_s, n_ctx, KVW))
            ya_s = _attn_dense(z_s, n_batch=n_s, seq=seq_s, tq=256, ctx=ctx, **offs)
            of_p, ob_p, sf_p, sb_p = _gla(z_p, up, bias, None, n_batch=n_p, seq=seq_p)
            s0 = state_gla[:, i].reshape(n_s, 2, B_HEADS * B_DK, B_DV)
            of_s, ob_s, _, _ = _gla(z_s, up, bias, (s0[:, 0], s0[:, 1]), n_batch=n_s, seq=seq_s)
            attn_k.append(z_p[:, E_K:E_K + KVW].reshape(n_p, seq_p, N_KV, HD))
            attn_v.append(z_p[:, E_V:E_V + KVW].reshape(n_p, seq_p, N_KV, HD))
            gla_s.append(jnp.stack([sf_p, sb_p], axis=1).reshape(n_p, 2, B_HEADS, B_DK, B_DV))
            g_off, onorm = E_BG, gla_onorm[i]
        else:
            w, wo = _prep_odd_weights(w_in_odd[i], w_out_odd[i])
            z_p = _inproj(y_p, mods_p[0], mods_p[1], g_pre[l], w, even=False, rows_per_batch=seq_p)
            z_s = _inproj(y_s, mods_s[0], mods_s[1], g_pre[l], w, even=False, rows_per_batch=seq_s,
                          rope_tabs=rope)
            sink = sink_c[i][sink_perm].astype(F32)
            ya_p = _attn_dense(z_p, n_batch=n_p, seq=seq_p, tq=seq_p, q_off=O_Q, g_off=O_G, k_off=O_K,
                               v_off=O_V, sink=sink)
            ctx = (cache_swa_k[:, i].reshape(n_s, n_ctx, KVW), cache_swa_v[:, i].reshape(n_s, n_ctx, KVW))
            ya_s = _attn_band(z_s, ctx, sink, n_batch=n_s, seq=seq_s)
            alog_row = _lane_row(a_log_d[i])
            dtb_row = _lane_row(dt_bias_d[i])
            qkv_p = _delta_prep(z_p, conv_d[i], seq=seq_p)
            qkv_s = _delta_prep(z_s, conv_d[i], seq=seq_s)
            of_p, ob_p, sf_p, sb_p = _delta(qkv_p, z_p, alog_row, dtb_row, None, n_batch=n_p, seq=seq_p)
            s0 = state_delta[:, i]
            of_s, ob_s, _, _ = _delta(qkv_s, z_s, alog_row, dtb_row, (s0[:, 0], s0[:, 1]), n_batch=n_s, seq=seq_s)
            swa_k.append(z_p[:, O_K:O_K + KVW].reshape(n_p, seq_p, N_KV, HD))
            swa_v.append(z_p[:, O_V:O_V + KVW].reshape(n_p, seq_p, N_KV, HD))
            delta_s.append(jnp.stack([sf_p, sb_p], axis=1))
            g_off, onorm = O_DG, delta_onorm[i]
        y_p = _outproj(ya_p, of_p, ob_p, z_p, g_off, onorm, wo, y_p, mods_p[2], g_post[l], rows_per_batch=seq_p)
        y_s = _outproj(ya_s, of_s, ob_s, z_s, g_off, onorm, wo, y_s, mods_s[2], g_post[l], rows_per_batch=seq_s)
    return (y_p.reshape(n_p, seq_p, D_MODEL), y_s.reshape(n_s, seq_s, D_MODEL),
            jnp.stack(attn_k, axis=1), jnp.stack(attn_v, axis=1), jnp.stack(gla_s, axis=1),
            jnp.stack(swa_k, axis=1), jnp.stack(swa_v, axis=1), jnp.stack(delta_s, axis=1))
```

```python
import functools

import numpy as np
import jax
import jax.numpy as jnp
from jax import lax
from jax.experimental import pallas as pl
from jax.experimental.pallas import tpu as pltpu

F32 = jnp.float32
BF16 = jnp.bfloat16

D_MODEL = 1024
DEPTH = 4
GRID_W = 64
HD = 64
N_QH = 8
N_KV = 2
QW = N_QH * HD
KVW = N_KV * HD
B_HEADS = 4
B_DK = 64
B_DV = 128
B_RANK = 16
GLA_NORMALIZER = 16.0
D_HEADS = 4
D_DK = 128
D_DV = 128
CONV_K = 5
WINDOW = 128
Q_BLOCK = 128
CHUNK = 64
ROPE_THETA = 10000.0
ROT_AXIS = HD // 2
EPS = 1e-6
LANE = 128
HALO = 8

ROW_TILE = 256
OUT_TILE = 512
REC_TILE = 256
VMEM_LIMIT = 48 * 1024 * 1024

E_Q, E_G, E_BV, E_BG, E_BQ, E_BK, E_K, E_V, E_R, E_W = 0, 512, 1024, 1536, 2048, 2304, 2560, 2688, 2816, 2944
O_DQKV, O_Q, O_G, O_DG, O_K, O_V, O_AB, O_W = 0, 1536, 2048, 2560, 3072, 3200, 3328, 3456
assert E_V == E_K + KVW and O_V == O_K + KVW


def _sigmoid(x):
    return 1.0 / (1.0 + jnp.exp(-x))


def _silu(x):
    return x * _sigmoid(x)


def _softplus(x):
    return jnp.maximum(x, 0.0) + jnp.log(1.0 + jnp.exp(-jnp.abs(x)))


def _dot(a, b):
    return jnp.dot(a.astype(BF16), b.astype(BF16), preferred_element_type=F32)


def _dot_nt(a, b):
    return lax.dot_general(a.astype(BF16), b.astype(BF16), (((1,), (1,)), ((), ())),
                           preferred_element_type=F32)


def _dot_tn(a, b):
    return lax.dot_general(a.astype(BF16), b.astype(BF16), (((0,), (0,)), ((), ())),
                           preferred_element_type=F32)


def _split(x):
    hi = x.astype(BF16)
    lo = (x - hi.astype(F32)).astype(BF16)
    return hi, lo


def _dot_exact_lhs(m, x):
    hi, lo = _split(x)
    return _dot(m, hi) + _dot(m, lo)


def _dot_tn_exact_rhs(x, m):
    hi, lo = _split(x)
    return _dot_tn(hi, m) + _dot_tn(lo, m)


def _iota(shape, dim):
    return lax.broadcasted_iota(jnp.int32, shape, dim)


def _cparams(sem):
    return pltpu.CompilerParams(dimension_semantics=sem, vmem_limit_bytes=VMEM_LIMIT)


def _mod_kernel(c_ref, w_ref, b_ref, o_ref):
    cs = _silu(c_ref[...])
    o_ref[0] = _dot(cs, w_ref[0]) + b_ref[0]


def _modulation(cond, w_mod, b_mod):
    tn = 1024
    return pl.pallas_call(
        _mod_kernel,
        grid=(DEPTH, 3 * D_MODEL // tn),
        in_specs=[
            pl.BlockSpec((8, D_MODEL), lambda l, j: (0, 0)),
            pl.BlockSpec((1, D_MODEL, tn), lambda l, j: (l, 0, j)),
            pl.BlockSpec((1, 1, tn), lambda l, j: (l, 0, j)),
        ],
        out_specs=pl.BlockSpec((1, 8, tn), lambda l, j: (l, 0, j)),
        out_shape=jax.ShapeDtypeStruct((DEPTH, 8, 3 * D_MODEL), F32),
        compiler_params=_cparams(("parallel", "parallel")),
        name="modulation",
    )(cond, w_mod, b_mod.reshape(DEPTH, 1, 3 * D_MODEL))


def _head_rms(x, bd, w):
    ss = _dot_exact_lhs_right(x * x, bd)
    return x * lax.rsqrt(ss * (1.0 / HD) + EPS) * w


def _dot_exact_lhs_right(x, m):
    hi, lo = _split(x)
    return _dot(hi, m) + _dot(lo, m)


def _rope(x, cos, sa, sb):
    return x * cos + pltpu.roll(x, LANE - ROT_AXIS // 2, axis=1) * sa + pltpu.roll(x, ROT_AXIS // 2, axis=1) * sb


def _inproj_kernel(*refs, even, rope):
    x_ref, shift_ref, scale_ref, g_ref, w_ref = refs[:5]
    pos = 5
    if even:
        qn_ref, kn_ref, bd_ref = refs[pos:pos + 3]
        pos += 3
    if rope:
        cos_ref, sa_ref, sb_ref = refs[pos:pos + 3]
        pos += 3
    o_ref = refs[pos]

    x = x_ref[...]
    y = x * lax.rsqrt(jnp.mean(x * x, axis=-1, keepdims=True) + EPS) * g_ref[...]
    h = (y * (1.0 + scale_ref[0]) + shift_ref[0]).astype(BF16)

    q_off, k_off = (E_Q, E_K) if even else (O_Q, O_K)
    width = E_W if even else O_W
    def transform(z, is_q):
        if even:
            z = _head_rms(z, bd_ref[...], qn_ref[...] if is_q else kn_ref[...])
        if rope:
            z = _rope(z, cos_ref[...], sa_ref[...], sb_ref[...])
        return z

    wide = 2 * LANE
    pending = []
    spans = [(0, width)]
    if even or rope:
        starts = [q_off + i * wide for i in range(QW // wide)] + [k_off]
        pending = [(off, jnp.dot(h, w_ref[:, off:off + wide], preferred_element_type=F32)) for off in starts]
        spans, cur = [], 0
        for off in sorted(starts):
            if off > cur:
                spans.append((cur, off))
            cur = off + wide
        if cur < width:
            spans.append((cur, width))
    for a, b in spans:
        step = 512
        for s in range(a, b, step):
            e = min(s + step, b)
            o_ref[:, s:e] = jnp.dot(h, w_ref[:, s:e], preferred_element_type=F32)
    for off, z in pending:
        if off == k_off:
            o_ref[:, off:off + LANE] = transform(z[:, :LANE], False)
            o_ref[:, off + LANE:off + wide] = z[:, LANE:]
        else:
            for i in range(wide // LANE):
                o_ref[:, off + i * LANE:off + (i + 1) * LANE] = transform(z[:, i * LANE:(i + 1) * LANE], True)


def _inproj(x, shift, scale, g_pre, w, *, even, rope_tabs=None, qn=None, kn=None, bd=None, rows_per_batch):
    T = x.shape[0]
    width = w.shape[1]
    tpb = rows_per_batch // ROW_TILE
    rope = rope_tabs is not None
    per_batch = shift.shape[0] > 1
    mod_idx = (lambda i: (i // tpb, 0, 0)) if per_batch else (lambda i: (0, 0, 0))
    ins = [x, shift, scale, g_pre.reshape(1, D_MODEL), w]
    specs = [
        pl.BlockSpec((ROW_TILE, D_MODEL), lambda i: (i, 0)),
        pl.BlockSpec((1, 1, D_MODEL), mod_idx),
        pl.BlockSpec((1, 1, D_MODEL), mod_idx),
        pl.BlockSpec((1, D_MODEL), lambda i: (0, 0)),
        pl.BlockSpec((D_MODEL, width), lambda i: (0, 0)),
    ]
    if even:
        ins += [qn, kn, bd]
        specs += [pl.BlockSpec((1, LANE), lambda i: (0, 0)),
                  pl.BlockSpec((1, LANE), lambda i: (0, 0)),
                  pl.BlockSpec((LANE, LANE), lambda i: (0, 0))]
    if rope:
        ins += list(rope_tabs)
        specs += [pl.BlockSpec((ROW_TILE, LANE), lambda i: (i % tpb, 0))] * 3
    return pl.pallas_call(
        functools.partial(_inproj_kernel, even=even, rope=rope),
        grid=(T // ROW_TILE,),
        in_specs=specs,
        out_specs=pl.BlockSpec((ROW_TILE, width), lambda i: (i, 0)),
        out_shape=jax.ShapeDtypeStruct((T, width), F32),
        compiler_params=_cparams(("parallel",)),
        name="inproj_even" if even else "inproj_odd",
    )(*ins)


LOG2E = 1.4426950408889634
SCORE_SCALE = HD ** -0.5 * LOG2E


def _values_with_ones(v):
    lower = _iota(v.shape, 1) < HD
    return jnp.where(lower, v, 1.0).astype(BF16), jnp.where(lower, 1.0, v).astype(BF16)


def _pipelined_heads(scores, finish, lookahead, lower, g_ref, o_ref):
    heads = [(c, e) for c in range(QW // LANE) for e in range(N_KV)]
    queue = [scores(*h) for h in heads[:lookahead]]
    halves = {}
    for i, (c, e) in enumerate(heads):
        cur = queue.pop(0)
        if i + lookahead < len(heads):
            queue.append(scores(*heads[i + lookahead]))
        halves[e] = finish(c, e, *cur)
        if e == N_KV - 1:
            oc = jnp.where(lower, halves[0], halves[1])
            o_ref[:, c * LANE:(c + 1) * LANE] = oc * _silu(g_ref[:, c * LANE:(c + 1) * LANE])


def _attn_dense_kernel(*refs, has_ctx, has_sink, tq, lookahead):
    q_ref, k_ref, v_ref, g_ref = refs[:4]
    pos = 4
    if has_ctx:
        kc_ref, vc_ref = refs[pos:pos + 2]
        pos += 2
    if has_sink:
        sink_ref = refs[pos]
        pos += 1
    o_ref, kb_ref, vb0_ref, vb1_ref = refs[pos:pos + 4]

    @pl.when(pl.program_id(1) == 0)
    def _():
        kb_ref[...] = k_ref[...].astype(BF16)
        vb0_ref[...], vb1_ref[...] = _values_with_ones(v_ref[...])

    kb = kb_ref[...]
    vb = (vb0_ref[...], vb1_ref[...])
    if has_ctx:
        kcb = kc_ref[0].astype(BF16)
        vcb = _values_with_ones(vc_ref[0])
    lower = _iota((tq, LANE), 1) < HD

    def scores(c, e):
        qc = q_ref[:, c * LANE:(c + 1) * LANE] * SCORE_SCALE
        qe = jnp.where(lower if e == 0 else jnp.logical_not(lower), qc, 0.0).astype(BF16)
        return _dot_nt(qe, kb), (_dot_nt(qe, kcb) if has_ctx else None)

    def finish(c, e, s, sc):
        m = jnp.max(s, axis=-1, keepdims=True)
        if has_ctx:
            m = jnp.maximum(m, jnp.max(sc, axis=-1, keepdims=True))
        if has_sink:
            sk = sink_ref[c * N_KV + e] * LOG2E
            m = jnp.maximum(m, sk)
        o = _dot(jnp.exp2(s - m), vb[e])
        if has_ctx:
            o = o + _dot(jnp.exp2(sc - m), vcb[e])
        l = pltpu.roll(o, HD, axis=1)
        if has_sink:
            l = l + jnp.exp2(sk - m)
        return o / l

    _pipelined_heads(scores, finish, lookahead, lower, g_ref, o_ref)


def _attn_dense(z, *, n_batch, seq, tq, q_off, g_off, k_off, v_off, ctx=None, sink=None):
    T = z.shape[0]
    nq = seq // tq
    has_ctx = ctx is not None
    has_sink = sink is not None
    ins = [z, z, z, z]
    specs = [
        pl.BlockSpec((tq, QW), lambda b, i: (b * nq + i, q_off // QW)),
        pl.BlockSpec((seq, KVW), lambda b, i: (b, k_off // KVW)),
        pl.BlockSpec((seq, KVW), lambda b, i: (b, v_off // KVW)),
        pl.BlockSpec((tq, QW), lambda b, i: (b * nq + i, g_off // QW)),
    ]
    if has_ctx:
        ins += list(ctx)
        n_ctx = ctx[0].shape[1]
        specs += [pl.BlockSpec((1, n_ctx, KVW), lambda b, i: (b, 0, 0))] * 2
    if has_sink:
        ins.append(sink)
        specs.append(pl.BlockSpec(memory_space=pltpu.SMEM))
    return pl.pallas_call(
        functools.partial(_attn_dense_kernel, has_ctx=has_ctx, has_sink=has_sink, tq=tq,
                          lookahead=N_QH if seq <= 1024 else 1),
        grid=(n_batch, nq),
        in_specs=specs,
        out_specs=pl.BlockSpec((tq, QW), lambda b, i: (b * nq + i, 0)),
        out_shape=jax.ShapeDtypeStruct((T, QW), F32),
        scratch_shapes=[pltpu.VMEM((seq, KVW), BF16)] * 3,
        compiler_params=_cparams(("arbitrary", "arbitrary")),
        name="attn_dense",
    )(*ins)


def _attn_band_kernel(q_ref, kp_ref, kc_ref, kn_ref, vp_ref, vc_ref, vn_ref, g_ref, kx_ref, vx_ref, sink_ref,
                      o_ref, *, seq):
    n = pl.program_id(1)
    tq = Q_BLOCK
    kcat = jnp.concatenate([kp_ref[...], kc_ref[...], kn_ref[...]], axis=0).astype(BF16)
    vcat = _values_with_ones(jnp.concatenate([vp_ref[...], vc_ref[...], vn_ref[...]], axis=0))
    kxb = kx_ref[0].astype(BF16)
    vxb = _values_with_ones(vx_ref[0])
    qpos = n * tq + _iota((tq, 3 * tq), 0)
    kpos = (n - 1) * tq + _iota((tq, 3 * tq), 1)
    valid = (jnp.abs(qpos - kpos) <= WINDOW) & (kpos >= 0) & (kpos < seq)
    lower = _iota((tq, LANE), 1) < HD

    def scores(c, e):
        qc = q_ref[:, c * LANE:(c + 1) * LANE] * SCORE_SCALE
        qe = jnp.where(lower if e == 0 else jnp.logical_not(lower), qc, 0.0).astype(BF16)
        return _dot_nt(qe, kcat), _dot_nt(qe, kxb)

    def finish(c, e, s, sx):
        s = jnp.where(valid, s, -jnp.inf)
        sk = sink_ref[c * N_KV + e] * LOG2E
        m = jnp.maximum(jnp.maximum(jnp.max(s, axis=-1, keepdims=True),
                                    jnp.max(sx, axis=-1, keepdims=True)), sk)
        o = _dot(jnp.exp2(s - m), vcat[e]) + _dot(jnp.exp2(sx - m), vxb[e])
        l = pltpu.roll(o, HD, axis=1) + jnp.exp2(sk - m)
        return o / l

    _pipelined_heads(scores, finish, N_QH, lower, g_ref, o_ref)


def _attn_band(z, ctx, sink, *, n_batch, seq):
    T = z.shape[0]
    nq = seq // Q_BLOCK
    n_ctx = ctx[0].shape[1]

    def kv_spec(off, delta):
        return pl.BlockSpec(
            (Q_BLOCK, KVW),
            lambda b, i: (b * nq + jnp.clip(i + delta, 0, nq - 1), off // KVW))

    specs = [
        pl.BlockSpec((Q_BLOCK, QW), lambda b, i: (b * nq + i, O_Q // QW)),
        kv_spec(O_K, -1), kv_spec(O_K, 0), kv_spec(O_K, 1),
        kv_spec(O_V, -1), kv_spec(O_V, 0), kv_spec(O_V, 1),
        pl.BlockSpec((Q_BLOCK, QW), lambda b, i: (b * nq + i, O_G // QW)),
        pl.BlockSpec((1, n_ctx, KVW), lambda b, i: (b, 0, 0)),
        pl.BlockSpec((1, n_ctx, KVW), lambda b, i: (b, 0, 0)),
        pl.BlockSpec(memory_space=pltpu.SMEM),
    ]
    return pl.pallas_call(
        functools.partial(_attn_band_kernel, seq=seq),
        grid=(n_batch, nq),
        in_specs=specs,
        out_specs=pl.BlockSpec((Q_BLOCK, QW), lambda b, i: (b * nq + i, 0)),
        out_shape=jax.ShapeDtypeStruct((T, QW), F32),
        compiler_params=_cparams(("parallel", "parallel")),
        name="attn_band",
    )(z, z, z, z, z, z, z, z, ctx[0], ctx[1], sink)


def _gla_tile(dirs, n_chunk):
    rows = n_chunk * CHUNK
    shift = CHUNK.bit_length() - 1
    ii = _iota((rows, rows), 0)
    jj = _iota((rows, rows), 1)
    same = lax.shift_right_logical(ii, shift) == lax.shift_right_logical(jj, shift)
    sel = jnp.where(lax.shift_right_logical(_iota((rows, n_chunk * B_DV), 0), shift)
                    == lax.shift_right_logical(_iota((rows, n_chunk * B_DV), 1), B_DV.bit_length() - 1),
                    1.0, 0.0).astype(BF16)
    ci = _iota((CHUNK, CHUNK), 0)
    cj = _iota((CHUNK, CHUNK), 1)
    lower = _iota((CHUNK, LANE), 1) < B_DK
    n_pair = B_HEADS * B_DK // LANE
    per_pair = LANE // B_DK

    gs = []
    for (_, _, _, r_ref, up, bias, _, _, _) in dirs:
        gk = _dot(r_ref[...], up) + bias
        gs.append(-_softplus(-gk) * (1.0 / GLA_NORMALIZER))
    bs = []
    for g, d in zip(gs, dirs):
        tri = jnp.where(same & ((jj >= ii) if d[8] else (jj <= ii)), 1.0, 0.0).astype(BF16)
        bs.append(_dot_exact_lhs(tri, g))
    decargs = [_dot_tn_exact_rhs(g, sel) for g in gs]

    units = []
    for di, (q_ref, k_ref, v_ref, _, _, _, st_ref, o_ref, rev) in enumerate(dirs):
        b = bs[di]
        incl = (cj >= ci) if rev else (cj <= ci)
        order = range(n_chunk - 1, -1, -1) if rev else range(n_chunk)
        for c in order:
            r0 = c * CHUNK
            bc = b[r0:r0 + CHUNK]
            btot = bc[0:1, :] if rev else bc[CHUNK - 1:CHUNK, :]
            qt = q_ref[r0:r0 + CHUNK, :] * (B_DK ** -0.5) * jnp.exp(bc)
            kc = k_ref[r0:r0 + CHUNK, :]
            kt = kc * jnp.exp(-bc)
            kend = kc * jnp.exp(btot - bc)
            for p in range(n_pair):
                for e in range(per_pair):
                    h = p * per_pair + e
                    qm = jnp.where(lower if e == 0 else jnp.logical_not(lower), qt[:, p * LANE:(p + 1) * LANE], 0.0)
                    units.append(dict(di=di, c=c, p=p, e=e, incl=incl, qm=qm.astype(BF16),
                                      kt=kt[:, p * LANE:(p + 1) * LANE].astype(BF16),
                                      kend=kend[:, p * LANE:(p + 1) * LANE].astype(BF16),
                                      v=v_ref[r0:r0 + CHUNK, h * B_DV:(h + 1) * B_DV].astype(BF16)))
    for u in units:
        u["att"] = jnp.where(u["incl"], _dot_nt(u["qm"], u["kt"]), 0.0).astype(BF16)
    for u in units:
        u["upd"] = _dot_tn(u["kend"], u["v"])[u["e"] * B_DK:(u["e"] + 1) * B_DK]
    for di, d in enumerate(dirs):
        st = d[6][...]
        mine = [u for u in units if u["di"] == di]
        for k0 in range(0, len(mine), B_HEADS):
            grp = mine[k0:k0 + B_HEADS]
            c = grp[0]["c"]
            for u in grp:
                u["st"] = st[u["p"] * LANE:(u["p"] + 1) * LANE].astype(BF16)
            dec = jnp.exp(decargs[di][:, c * B_DV:(c + 1) * B_DV])
            st = dec * st + jnp.concatenate([u["upd"] for u in grp], axis=0)
        d[6][...] = st
    for u in units:
        u["qs"] = _dot(u["qm"], u["st"])
    for u in units:
        u["o"] = _dot(u["att"], u["v"]) + u["qs"]
    for di, d in enumerate(dirs):
        mine = [u for u in units if u["di"] == di]
        for k0 in range(0, len(mine), B_HEADS):
            grp = mine[k0:k0 + B_HEADS]
            r0 = grp[0]["c"] * CHUNK
            d[7][r0:r0 + CHUNK, :] = jnp.concatenate([u["o"] for u in grp], axis=1)


def _gla_kernel(*refs, has_s0, n_chunk):
    qf, kf, vf, rf, qb, kb, vb, rb, up_ref, bias_ref = refs[:10]
    pos = 10
    if has_s0:
        s0f, s0b = refs[pos:pos + 2]
        pos += 2
    of_ref, ob_ref, sf_ref, sb_ref, stf, stb = refs[pos:pos + 6]
    n = pl.program_id(1)

    @pl.when(n == 0)
    def _():
        if has_s0:
            stf[...] = s0f[0]
            stb[...] = s0b[0]
        else:
            stf[...] = jnp.zeros_like(stf)
            stb[...] = jnp.zeros_like(stb)

    _gla_tile([(qf, kf, vf, rf, up_ref[0], bias_ref[0], stf, of_ref, False),
               (qb, kb, vb, rb, up_ref[1], bias_ref[1], stb, ob_ref, True)], n_chunk)

    @pl.when(n == pl.num_programs(1) - 1)
    def _():
        sf_ref[0] = stf[...]
        sb_ref[0] = stb[...]


def _gla(z, up_ext, bias, s0, *, n_batch, seq):
    T = z.shape[0]
    nt = seq // REC_TILE
    has_s0 = s0 is not None
    sw = B_HEADS * B_DK

    def fwd(b, i):
        return b * nt + i

    def bwd(b, i):
        return b * nt + nt - 1 - i

    def specs_for(row):
        return [
            pl.BlockSpec((REC_TILE, sw), lambda b, i: (row(b, i), E_BQ // sw)),
            pl.BlockSpec((REC_TILE, sw), lambda b, i: (row(b, i), E_BK // sw)),
            pl.BlockSpec((REC_TILE, QW), lambda b, i: (row(b, i), E_BV // QW)),
            pl.BlockSpec((REC_TILE, LANE), lambda b, i: (row(b, i), E_R // LANE)),
        ]

    ins = [z] * 8 + [up_ext, bias]
    specs = specs_for(fwd) + specs_for(bwd) + [
        pl.BlockSpec((2, LANE, sw), lambda b, i: (0, 0, 0)),
        pl.BlockSpec((2, 1, sw), lambda b, i: (0, 0, 0)),
    ]
    if has_s0:
        ins += [s0[0], s0[1]]
        specs += [pl.BlockSpec((1, sw, B_DV), lambda b, i: (b, 0, 0))] * 2
    st_shape = jax.ShapeDtypeStruct((n_batch, sw, B_DV), F32)
    o_shape = jax.ShapeDtypeStruct((T, QW), F32)
    return pl.pallas_call(
        functools.partial(_gla_kernel, has_s0=has_s0, n_chunk=REC_TILE // CHUNK),
        grid=(n_batch, nt),
        in_specs=specs,
        out_specs=[
            pl.BlockSpec((REC_TILE, QW), lambda b, i: (fwd(b, i), 0)),
            pl.BlockSpec((REC_TILE, QW), lambda b, i: (bwd(b, i), 0)),
            pl.BlockSpec((1, sw, B_DV), lambda b, i: (b, 0, 0)),
            pl.BlockSpec((1, sw, B_DV), lambda b, i: (b, 0, 0)),
        ],
        out_shape=[o_shape, o_shape, st_shape, st_shape],
        scratch_shapes=[pltpu.VMEM((sw, B_DV), F32), pltpu.VMEM((sw, B_DV), F32)],
        compiler_params=_cparams(("arbitrary", "arbitrary")),
        name="gla",
    )(*ins)


def _delta_prep_kernel(cur_ref, prev_ref, next_ref, w_ref, o_ref, xe_ref, *, tiles_per_batch):
    i = pl.program_id(0)
    first = (i % tiles_per_batch) == 0
    last = (i % tiles_per_batch) == tiles_per_batch - 1
    xe_ref[0:HALO, :] = jnp.where(first, 0.0, prev_ref[...])
    xe_ref[HALO:HALO + REC_TILE, :] = cur_ref[...]
    xe_ref[HALO + REC_TILE:, :] = jnp.where(last, 0.0, next_ref[...])
    acc = None
    for j in range(CONV_K):
        start = HALO - CONV_K // 2 + j
        term = xe_ref[start:start + REC_TILE, :] * w_ref[j:j + 1, :]
        acc = term if acc is None else acc + term
    y = _silu(acc)
    qk_w = 2 * D_HEADS * D_DK
    for h in range(2 * D_HEADS):
        seg = y[:, h * D_DK:(h + 1) * D_DK]
        nrm = seg * lax.rsqrt(jnp.sum(seg * seg, axis=-1, keepdims=True) + EPS)
        if h < D_HEADS:
            nrm = nrm * (D_DK ** -0.5)
        o_ref[:, h * D_DK:(h + 1) * D_DK] = nrm
    o_ref[:, qk_w:] = y[:, qk_w:]


def _delta_prep(z, conv_w, *, seq):
    T = z.shape[0]
    tpb = seq // REC_TILE
    cw = 3 * D_HEADS * D_DK
    hb = REC_TILE // HALO
    n_hb = T // HALO
    return pl.pallas_call(
        functools.partial(_delta_prep_kernel, tiles_per_batch=tpb),
        grid=(T // REC_TILE,),
        in_specs=[
            pl.BlockSpec((REC_TILE, cw), lambda i: (i, 0)),
            pl.BlockSpec((HALO, cw), lambda i: (jnp.maximum(i * hb - 1, 0), 0)),
            pl.BlockSpec((HALO, cw), lambda i: (jnp.minimum((i + 1) * hb, n_hb - 1), 0)),
            pl.BlockSpec((CONV_K, cw), lambda i: (0, 0)),
        ],
        out_specs=pl.BlockSpec((REC_TILE, cw), lambda i: (i, 0)),
        out_shape=jax.ShapeDtypeStruct((T, cw), F32),
        scratch_shapes=[pltpu.VMEM((REC_TILE + 2 * HALO, cw), F32)],
        compiler_params=_cparams(("parallel",)),
        name="delta_prep",
    )(z, z, z, conv_w)


def _dot3_stacked(lhs_list, rhs, expand=None):
    rh, rl = _split(rhs)
    if expand is not None:
        rh, rl = expand(rh), expand(rl)
    parts = [_split(x) for x in lhs_list]
    his = [p[0] for p in parts]
    los = [p[1] for p in parts]
    n = lhs_list[0].shape[0]
    k = len(lhs_list)
    r1 = _dot(jnp.concatenate(his + los, axis=0), rh)
    r2 = _dot(jnp.concatenate(his, axis=0) if k > 1 else his[0], rl)
    return [r1[i * n:(i + 1) * n] + (r1[(k + i) * n:(k + i + 1) * n] + r2[i * n:(i + 1) * n]) for i in range(k)]


def _delta_tile(dirs, alog, dtb, n_chunk):
    C = CHUNK
    W = D_HEADS * C
    hk = D_HEADS * D_DK
    sh = C.bit_length() - 1
    ii = _iota((C, C), 0)
    jj = _iota((C, C), 1)
    pi = _iota((C, W), 0)
    pl_ = _iota((C, W), 1)
    pj = pl_ & (C - 1)
    pblk = lax.shift_right_logical(pl_, sh)
    eye_p = jnp.where(pi == pj, 1.0, 0.0)
    bdm = (lax.shift_right_logical(_iota((W, W), 0), sh) == lax.shift_right_logical(_iota((W, W), 1), sh))
    bdk = (lax.shift_right_logical(_iota((W, hk), 0), sh)
           == lax.shift_right_logical(_iota((W, hk), 1), D_DK.bit_length() - 1))
    ones_cc = jnp.ones((C, C), BF16)

    def pack(cols):
        out = jnp.broadcast_to(cols[D_HEADS - 1], (C, W))
        for h in range(D_HEADS - 2, -1, -1):
            out = jnp.where(pblk == h, jnp.broadcast_to(cols[h], (C, W)), out)
        return out

    def wide(cols):
        return jnp.concatenate([jnp.broadcast_to(c, (C, D_DK)) for c in cols], axis=1)

    def block_diag(x):
        return jnp.where(bdm, jnp.concatenate([x] * D_HEADS, axis=0), jnp.zeros((), x.dtype))

    units = []
    for d, (x_ref, ab_ref, st_ref, o_ref, rev) in enumerate(dirs):
        incl = (jj >= ii) if rev else (jj <= ii)
        tri = jnp.where(incl, 1.0, 0.0).astype(BF16)
        incl_p = (pj >= pi) if rev else (pj <= pi)
        strict_p = (pj > pi) if rev else (pj < pi)
        before_p = (pj <= pi) if rev else (pj >= pi)
        order = range(n_chunk - 1, -1, -1) if rev else range(n_chunk)
        for c in order:
            r0 = c * C
            ab = ab_ref[r0:r0 + C, :]
            gall = -jnp.exp(alog) * _softplus(ab + dtb)
            beta_all = _sigmoid(ab)
            gc_all = _dot_exact_lhs(tri, gall)
            ig = [d * D_HEADS + h for h in range(D_HEADS)]
            gcols = [gc_all[:, i:i + 1] for i in ig]
            g_p = pack([gall[:, i:i + 1] for i in ig])
            gc_p = pack(gcols)
            gr_p = _dot_exact_lhs(ones_cc, jnp.where(before_p, g_p, 0.0))
            decay = jnp.exp(jnp.where(incl_p, gc_p - gr_p, -jnp.inf))
            gc_w = wide(gcols)
            beta_w = wide([beta_all[:, 2 * D_HEADS + i:2 * D_HEADS + i + 1] for i in ig])
            glast_w = gc_w[0:1, :] if rev else gc_w[C - 1:C, :]
            egc_w = jnp.exp(gc_w)
            q_all = x_ref[r0:r0 + C, 0:hk]
            k_all = x_ref[r0:r0 + C, hk:2 * hk]
            v_all = x_ref[r0:r0 + C, 2 * hk:3 * hk]
            kb_all = k_all * beta_w
            k_bd = jnp.where(bdk, jnp.concatenate([k_all] * D_HEADS, axis=0), 0.0)
            r = _dot_nt(jnp.concatenate([kb_all, q_all], axis=0), k_bd)
            p = -jnp.where(strict_p, r[:C] * decay, 0.0)
            vb_all = v_all * beta_w
            kbe_all = kb_all * egc_w
            rhs = jnp.concatenate(
                [jnp.concatenate([vb_all[:, h * D_DV:(h + 1) * D_DV], kbe_all[:, h * D_DK:(h + 1) * D_DK]], axis=1)
                 for h in range(D_HEADS)], axis=0)
            units.append(dict(d=d, r0=r0, st_ref=st_ref, o_ref=o_ref, p=p, acc=eye_p + p, att=r[C:] * decay,
                              rhs=rhs, qdec=q_all * egc_w, kdec=k_all * jnp.exp(glast_w - gc_w),
                              eglast=jnp.exp(glast_w)))
    for u in units:
        u["p"] = _dot3_stacked([u["p"]], u["p"], block_diag)[0]
    m = 2
    while m < C:
        last = 2 * m >= C
        for u in units:
            if last:
                u["acc"] = u["acc"] + _dot3_stacked([u["acc"]], u["p"], block_diag)[0]
            else:
                pp, ap = _dot3_stacked([u["p"], u["acc"]], u["p"], block_diag)
                u["acc"] = u["acc"] + ap
                u["p"] = pp
        m *= 2
    per_pair = LANE // C
    half = lax.shift_right_logical(_iota((C, LANE), 1), sh)
    for u in units:
        u["sol"] = []
        for c in range(D_HEADS // per_pair):
            lanes = slice(c * LANE, (c + 1) * LANE)
            lhs = [jnp.where(half == e, u["acc"][:, lanes], 0.0) for e in range(per_pair)]
            u["sol"] += _dot3_stacked(lhs, u["rhs"][c * LANE:(c + 1) * LANE])
    per_dir = [[u for u in units if u["d"] == d] for d in range(len(dirs))]
    for step in range(n_chunk):
        cur = [lst[step] for lst in per_dir]
        for u in cur:
            u["s"] = [u["st_ref"][h] for h in range(D_HEADS)]
            u["vnew"], u["qs"] = [], []
            for h in range(D_HEADS):
                lhs = jnp.concatenate([u["sol"][h][:, D_DV:], u["qdec"][:, h * D_DK:(h + 1) * D_DK]], axis=0)
                r = _dot(lhs, u["s"][h])
                u["vnew"].append(u["sol"][h][:, :D_DV] - r[:C])
                u["qs"].append(r[C:])
        for u in cur:
            o_pairs = []
            for c in range(D_HEADS // per_pair):
                lanes = slice(c * LANE, (c + 1) * LANE)
                att_rows = jnp.concatenate(
                    [jnp.where(half == e, u["att"][:, lanes], 0.0) for e in range(per_pair)],
                    axis=0)
                o_pairs.append(_dot(att_rows, jnp.concatenate(u["vnew"][c * per_pair:(c + 1) * per_pair], axis=0)))
            o_all = jnp.concatenate(o_pairs, axis=0)
            outs = []
            for h in range(D_HEADS):
                outs.append(u["qs"][h] + o_all[h * C:(h + 1) * C])
                u["st_ref"][h] = (u["s"][h] * u["eglast"][:, h * D_DK:(h + 1) * D_DK]
                                  + _dot_tn(u["kdec"][:, h * D_DK:(h + 1) * D_DK], u["vnew"][h]))
            u["o_ref"][u["r0"]:u["r0"] + C, :] = jnp.concatenate(outs, axis=1)


def _delta_kernel(*refs, has_s0, n_chunk):
    xf, af, xb, ab_, alog_ref, dtb_ref = refs[:6]
    pos = 6
    if has_s0:
        s0f, s0b = refs[pos:pos + 2]
        pos += 2
    of_ref, ob_ref, sf_ref, sb_ref, stf, stb = refs[pos:pos + 6]
    n = pl.program_id(1)

    @pl.when(n == 0)
    def _():
        if has_s0:
            stf[...] = s0f[0]
            stb[...] = s0b[0]
        else:
            stf[...] = jnp.zeros_like(stf)
            stb[...] = jnp.zeros_like(stb)

    _delta_tile([(xf, af, stf, of_ref, False), (xb, ab_, stb, ob_ref, True)],
                alog_ref[...], dtb_ref[...], n_chunk)

    @pl.when(n == pl.num_programs(1) - 1)
    def _():
        sf_ref[0] = stf[...]
        sb_ref[0] = stb[...]


def _delta(qkv, z, alog_row, dtb_row, s0, *, n_batch, seq):
    T = z.shape[0]
    nt = seq // REC_TILE
    has_s0 = s0 is not None
    cw = 3 * D_HEADS * D_DK

    def fwd(b, i):
        return b * nt + i

    def bwd(b, i):
        return b * nt + nt - 1 - i

    ins = [qkv, z, qkv, z, alog_row, dtb_row]
    specs = [
        pl.BlockSpec((REC_TILE, cw), lambda b, i: (fwd(b, i), 0)),
        pl.BlockSpec((REC_TILE, LANE), lambda b, i: (fwd(b, i), O_AB // LANE)),
        pl.BlockSpec((REC_TILE, cw), lambda b, i: (bwd(b, i), 0)),
        pl.BlockSpec((REC_TILE, LANE), lambda b, i: (bwd(b, i), O_AB // LANE)),
        pl.BlockSpec((1, LANE), lambda b, i: (0, 0)),
        pl.BlockSpec((1, LANE), lambda b, i: (0, 0)),
    ]
    st_block = (1, D_HEADS, D_DK, D_DV)
    if has_s0:
        ins += [s0[0], s0[1]]
        specs += [pl.BlockSpec(st_block, lambda b, i: (b, 0, 0, 0))] * 2
    st_shape = jax.ShapeDtypeStruct((n_batch, D_HEADS, D_DK, D_DV), F32)
    o_shape = jax.ShapeDtypeStruct((T, QW), F32)
    return pl.pallas_call(
        functools.partial(_delta_kernel, has_s0=has_s0, n_chunk=REC_TILE // CHUNK),
        grid=(n_batch, nt),
        in_specs=specs,
        out_specs=[
            pl.BlockSpec((REC_TILE, QW), lambda b, i: (fwd(b, i), 0)),
            pl.BlockSpec((REC_TILE, QW), lambda b, i: (bwd(b, i), 0)),
            pl.BlockSpec(st_block, lambda b, i: (b, 0, 0, 0)),
            pl.BlockSpec(st_block, lambda b, i: (b, 0, 0, 0)),
        ],
        out_shape=[o_shape, o_shape, st_shape, st_shape],
        scratch_shapes=[pltpu.VMEM((D_HEADS, D_DK, D_DV), F32), pltpu.VMEM((D_HEADS, D_DK, D_DV), F32)],
        compiler_params=_cparams(("arbitrary", "arbitrary")),
        name="delta",
    )(*ins)


def _outproj_kernel(ya_ref, of_ref, ob_ref, zg_ref, on_ref, w_ref, x_ref, gate_ref, gp_ref, o_ref):
    ob = of_ref[...] + ob_ref[...]
    zg = zg_ref[...]
    on = on_ref[...]
    parts = []
    for h in range(QW // LANE):
        seg = ob[:, h * LANE:(h + 1) * LANE]
        nrm = seg * lax.rsqrt(jnp.mean(seg * seg, axis=-1, keepdims=True) + EPS) * on
        parts.append(nrm * _silu(zg[:, h * LANE:(h + 1) * LANE]))
    yb = jnp.concatenate(parts, axis=1)
    out = _dot(ya_ref[...], w_ref[0:QW, :]) + _dot(yb, w_ref[QW:, :])
    post = out * lax.rsqrt(jnp.mean(out * out, axis=-1, keepdims=True) + EPS) * gp_ref[...]
    o_ref[...] = x_ref[...] + gate_ref[0] * post


def _outproj(ya, o_f, o_b, z, g_off, onorm, w_out, x, gate, g_post, *, rows_per_batch):
    T = x.shape[0]
    tpb = rows_per_batch // OUT_TILE
    row = lambda i: (i, 0)
    per_batch = gate.shape[0] > 1
    mod_idx = (lambda i: (i // tpb, 0, 0)) if per_batch else (lambda i: (0, 0, 0))
    return pl.pallas_call(
        _outproj_kernel,
        grid=(T // OUT_TILE,),
        in_specs=[
            pl.BlockSpec((OUT_TILE, QW), row),
            pl.BlockSpec((OUT_TILE, QW), row),
            pl.BlockSpec((OUT_TILE, QW), row),
            pl.BlockSpec((OUT_TILE, QW), lambda i: (i, g_off // QW)),
            pl.BlockSpec((1, LANE), lambda i: (0, 0)),
            pl.BlockSpec((2 * QW, D_MODEL), lambda i: (0, 0)),
            pl.BlockSpec((OUT_TILE, D_MODEL), row),
            pl.BlockSpec((1, 1, D_MODEL), mod_idx),
            pl.BlockSpec((1, D_MODEL), lambda i: (0, 0)),
        ],
        out_specs=pl.BlockSpec((OUT_TILE, D_MODEL), row),
        out_shape=jax.ShapeDtypeStruct((T, D_MODEL), F32),
        compiler_params=_cparams(("parallel",)),
        name="outproj",
    )(ya, o_f, o_b, z, onorm.reshape(1, LANE), w_out, x, gate, g_post.reshape(1, D_MODEL))


def _rope_tables(seq):
    n_rows = seq // GRID_W
    rows = jnp.repeat(jnp.arange(n_rows, dtype=F32), GRID_W)
    cols = jnp.tile(jnp.arange(GRID_W, dtype=F32), n_rows)
    inv = jnp.power(ROPE_THETA, jnp.arange(ROT_AXIS // 2, dtype=F32) * (-2.0 / ROT_AXIS))
    ang_r = rows[:, None] * inv[None, :]
    ang_c = cols[:, None] * inv[None, :]
    zero = jnp.zeros_like(ang_r)
    cos = jnp.concatenate([jnp.cos(ang_r)] * 2 + [jnp.cos(ang_c)] * 2, axis=1)
    sa = jnp.concatenate([-jnp.sin(ang_r), zero, -jnp.sin(ang_c), zero], axis=1)
    sb = jnp.concatenate([zero, jnp.sin(ang_r), zero, jnp.sin(ang_c)], axis=1)
    return tuple(jnp.tile(t, (1, LANE // HD)) for t in (cos, sa, sb))


def _regroup_cols(w):
    g = N_QH // N_KV
    lead = w.shape[:-1]
    return jnp.swapaxes(w.reshape(lead + (N_KV, g, HD)), -3, -2).reshape(lead + (QW,))


def _regroup_rows(w):
    g = N_QH // N_KV
    return jnp.swapaxes(w.reshape(w.shape[:-2] + (N_KV, g, HD, w.shape[-1])), -4, -3).reshape(w.shape)


def _prep_even_weights(w_in, w_out, gk_up):
    n = w_in.shape[0]
    w_in = w_in.astype(BF16)
    a_q, a_k, a_v, a_g, b_q, b_k, b_v, b_r, b_g = jnp.split(
        w_in, [512, 640, 768, 1280, 1536, 1792, 2304, 2336], axis=2)
    pad = jnp.zeros((n, D_MODEL, LANE - 2 * B_RANK), BF16)
    w = jnp.concatenate([_regroup_cols(a_q), _regroup_cols(a_g), b_v, b_g, b_q, b_k, a_k, a_v, b_r, pad], axis=2)
    w_out = w_out.astype(BF16)
    wo = jnp.concatenate([_regroup_rows(w_out[:, :QW]), w_out[:, QW:]], axis=1)
    zeros = jnp.zeros((n, B_RANK, B_HEADS * B_DK), BF16)
    tail = jnp.zeros((n, LANE - 2 * B_RANK, B_HEADS * B_DK), BF16)
    gk_up = gk_up.astype(BF16)
    up = jnp.stack([jnp.concatenate([gk_up[:, 0], zeros, tail], axis=1),
                    jnp.concatenate([zeros, gk_up[:, 1], tail], axis=1)], axis=1)
    return w, wo, up


def _prep_odd_weights(w_in, w_out):
    n = w_in.shape[0]
    w_in = w_in.astype(BF16)
    c_q, c_k, c_v, c_g, d_q, d_k, d_v, d_a, d_b, d_g = jnp.split(
        w_in, [512, 640, 768, 1280, 1792, 2304, 2816, 2824, 2832], axis=2)
    pad = jnp.zeros((n, D_MODEL, LANE - 4 * D_HEADS), BF16)
    w = jnp.concatenate([d_q, d_k, d_v, _regroup_cols(c_q), _regroup_cols(c_g), d_g, c_k, c_v, d_a, d_b, pad],
                        axis=2)
    w_out = w_out.astype(BF16)
    wo = jnp.concatenate([_regroup_rows(w_out[:, :QW]), w_out[:, QW:]], axis=1)
    return w, wo


def _lane_row(x):
    flat = x.reshape(-1).astype(F32)
    return jnp.zeros((1, LANE), F32).at[0, :flat.shape[0]].set(flat)


def kernel(x_prompt, x_sample, c, cache_attn_k, cache_attn_v, state_gla, cache_swa_k, cache_swa_v, state_delta,
           c_ctx, w_mod, b_mod, g_pre, g_post, w_in_even, w_out_even, qnorm_a, knorm_a, gla_gk_up, gla_gk_bias,
           gla_onorm, w_in_odd, w_out_odd, sink_c, conv_d, a_log_d, dt_bias_d, delta_onorm):
    n_p, seq_p, _ = x_prompt.shape
    n_s, seq_s, _ = x_sample.shape
    n_ctx = cache_attn_k.shape[2]
    assert (n_p * seq_p) % OUT_TILE == 0 and seq_s % OUT_TILE == 0 and seq_p % ROW_TILE == 0
    assert seq_s % ROW_TILE == 0 and seq_p % REC_TILE == 0 and seq_s % REC_TILE == 0
    assert n_s + 1 <= 8

    cond = jnp.zeros((8, D_MODEL), F32).at[0].set(c_ctx).at[1:1 + n_s].set(c)
    mod = _modulation(cond, w_mod, b_mod)
    rope = _rope_tables(seq_s)
    bd = jnp.asarray(np.kron(np.eye(LANE // HD), np.ones((HD, HD))), BF16)
    w_even, wo_even, up_even = _prep_even_weights(w_in_even, w_out_even, gla_gk_up)
    w_odd, wo_odd = _prep_odd_weights(w_in_odd, w_out_odd)
    sinks = jnp.swapaxes(sink_c.astype(F32).reshape(-1, N_KV, N_QH // N_KV), 1, 2).reshape(-1, N_QH)

    y_p = x_prompt.reshape(n_p * seq_p, D_MODEL)
    y_s = x_sample.reshape(n_s * seq_s, D_MODEL)
    attn_k, attn_v, gla_s, swa_k, swa_v, delta_s = [], [], [], [], [], []
    for l in range(DEPTH):
        i = l // 2
        shift, scale, gate = [mod[l, :, j * D_MODEL:(j + 1) * D_MODEL].reshape(8, 1, D_MODEL) for j in range(3)]
        mods_p = (shift[0:1], scale[0:1], gate[0:1])
        mods_s = (shift[1:1 + n_s], scale[1:1 + n_s], gate[1:1 + n_s])
        if l % 2 == 0:
            w, wo, up = w_even[i], wo_even[i], up_even[i]
            qn = jnp.tile(qnorm_a[i], LANE // HD).reshape(1, LANE)
            kn = jnp.tile(knorm_a[i], LANE // HD).reshape(1, LANE)
            bias = gla_gk_bias[i].reshape(2, 1, B_HEADS * B_DK)
            common = dict(even=True, qn=qn, kn=kn, bd=bd)
            z_p = _inproj(y_p, mods_p[0], mods_p[1], g_pre[l], w, rows_per_batch=seq_p, **common)
            z_s = _inproj(y_s, mods_s[0], mods_s[1], g_pre[l], w, rows_per_batch=seq_s, rope_tabs=rope, **common)
            offs = dict(q_off=E_Q, g_off=E_G, k_off=E_K, v_off=E_V)
            ya_p = _attn_dense(z_p, n_batch=n_p, seq=seq_p, tq=seq_p, **offs)
            ctx = (cache_attn_k[:, i].reshape(n_s, n_ctx, KVW), cache_attn_v[:, i].reshape(n_s, n_ctx, KVW))
            ya_s = _attn_dense(z_s, n_batch=n_s, seq=seq_s, tq=256, ctx=ctx, **offs)
            of_p, ob_p, sf_p, sb_p = _gla(z_p, up, bias, None, n_batch=n_p, seq=seq_p)
            s0 = state_gla[:, i].reshape(n_s, 2, B_HEADS * B_DK, B_DV)
            of_s, ob_s, _, _ = _gla(z_s, up, bias, (s0[:, 0], s0[:, 1]), n_batch=n_s, seq=seq_s)
            attn_k.append(z_p[:, E_K:E_K + KVW].reshape(n_p, seq_p, N_KV, HD))
            attn_v.append(z_p[:, E_V:E_V + KVW].reshape(n_p, seq_p, N_KV, HD))
            gla_s.append(jnp.stack([sf_p, sb_p], axis=1).reshape(n_p, 2, B_HEADS, B_DK, B_DV))
            g_off, onorm = E_BG, gla_onorm[i]
        else:
            w, wo = w_odd[i], wo_odd[i]
            z_p = _inproj(y_p, mods_p[0], mods_p[1], g_pre[l], w, even=False, rows_per_batch=seq_p)
            z_s = _inproj(y_s, mods_s[0], mods_s[1], g_pre[l], w, even=False, rows_per_batch=seq_s,
                          rope_tabs=rope)
            sink = sinks[i]
            ya_p = _attn_dense(z_p, n_batch=n_p, seq=seq_p, tq=seq_p, q_off=O_Q, g_off=O_G, k_off=O_K,
                               v_off=O_V, sink=sink)
            ctx = (cache_swa_k[:, i].reshape(n_s, n_ctx, KVW), cache_swa_v[:, i].reshape(n_s, n_ctx, KVW))
            ya_s = _attn_band(z_s, ctx, sink, n_batch=n_s, seq=seq_s)
            alog_row = _lane_row(a_log_d[i])
            dtb_row = _lane_row(dt_bias_d[i])
            qkv_p = _delta_prep(z_p, conv_d[i], seq=seq_p)
            qkv_s = _delta_prep(z_s, conv_d[i], seq=seq_s)
            of_p, ob_p, sf_p, sb_p = _delta(qkv_p, z_p, alog_row, dtb_row, None, n_batch=n_p, seq=seq_p)
            s0 = state_delta[:, i]
            of_s, ob_s, _, _ = _delta(qkv_s, z_s, alog_row, dtb_row, (s0[:, 0], s0[:, 1]), n_batch=n_s, seq=seq_s)
            swa_k.append(z_p[:, O_K:O_K + KVW].reshape(n_p, seq_p, N_KV, HD))
            swa_v.append(z_p[:, O_V:O_V + KVW].reshape(n_p, seq_p, N_KV, HD))
            delta_s.append(jnp.stack([sf_p, sb_p], axis=1))
            g_off, onorm = O_DG, delta_onorm[i]
        y_p = _outproj(ya_p, of_p, ob_p, z_p, g_off, onorm, wo, y_p, mods_p[2], g_post[l], rows_per_batch=seq_p)
        y_s = _outproj(ya_s, of_s, ob_s, z_s, g_off, onorm, wo, y_s, mods_s[2], g_post[l], rows_per_batch=seq_s)
    return (y_p.reshape(n_p, seq_p, D_MODEL), y_s.reshape(n_s, seq_s, D_MODEL),
            jnp.stack(attn_k, axis=1), jnp.stack(attn_v, axis=1), jnp.stack(gla_s, axis=1),
            jnp.stack(swa_k, axis=1), jnp.stack(swa_v, axis=1), jnp.stack(delta_s, axis=1))
```

```python
import functools

import numpy as np
import jax
import jax.numpy as jnp
from jax import lax
from jax.experimental import pallas as pl
from jax.experimental.pallas import tpu as pltpu

F32 = jnp.float32
BF16 = jnp.bfloat16

D_MODEL = 1024
DEPTH = 4
GRID_W = 64
HD = 64
N_QH = 8
N_KV = 2
QW = N_QH * HD
KVW = N_KV * HD
B_HEADS = 4
B_DK = 64
B_DV = 128
B_RANK = 16
GLA_NORMALIZER = 16.0
D_HEADS = 4
D_DK = 128
D_DV = 128
CONV_K = 5
WINDOW = 128
Q_BLOCK = 128
CHUNK = 64
ROPE_THETA = 10000.0
ROT_AXIS = HD // 2
EPS = 1e-6
LANE = 128
HALO = 8

ROW_TILE = 256
OUT_TILE = 512
REC_TILE = 256
VMEM_LIMIT = 48 * 1024 * 1024

E_Q, E_G, E_BV, E_BG, E_BQ, E_BK, E_K, E_V, E_R, E_W = 0, 512, 1024, 1536, 2048, 2304, 2560, 2688, 2816, 2944
O_DQKV, O_Q, O_G, O_DG, O_K, O_V, O_AB, O_W = 0, 1536, 2048, 2560, 3072, 3200, 3328, 3456
assert E_V == E_K + KVW and O_V == O_K + KVW


def _sigmoid(x):
    return 1.0 / (1.0 + jnp.exp(-x))


def _silu(x):
    return x * _sigmoid(x)


def _softplus(x):
    return jnp.maximum(x, 0.0) + jnp.log(1.0 + jnp.exp(-jnp.abs(x)))


def _dot(a, b):
    return jnp.dot(a.astype(BF16), b.astype(BF16), preferred_element_type=F32)


def _dot_nt(a, b):
    return lax.dot_general(a.astype(BF16), b.astype(BF16), (((1,), (1,)), ((), ())),
                           preferred_element_type=F32)


def _dot_tn(a, b):
    return lax.dot_general(a.astype(BF16), b.astype(BF16), (((0,), (0,)), ((), ())),
                           preferred_element_type=F32)


def _split(x):
    hi = x.astype(BF16)
    lo = (x - hi.astype(F32)).astype(BF16)
    return hi, lo


def _dot_exact_lhs(m, x):
    hi, lo = _split(x)
    return _dot(m, hi) + _dot(m, lo)


def _dot_tn_exact_rhs(x, m):
    hi, lo = _split(x)
    return _dot_tn(hi, m) + _dot_tn(lo, m)


def _iota(shape, dim):
    return lax.broadcasted_iota(jnp.int32, shape, dim)


def _cparams(sem):
    return pltpu.CompilerParams(dimension_semantics=sem, vmem_limit_bytes=VMEM_LIMIT)


def _mod_kernel(c_ref, w_ref, b_ref, o_ref):
    cs = _silu(c_ref[...])
    o_ref[0] = _dot(cs, w_ref[0]) + b_ref[0]


def _modulation(cond, w_mod, b_mod):
    tn = 1024
    return pl.pallas_call(
        _mod_kernel,
        grid=(DEPTH, 3 * D_MODEL // tn),
        in_specs=[
            pl.BlockSpec((8, D_MODEL), lambda l, j: (0, 0)),
            pl.BlockSpec((1, D_MODEL, tn), lambda l, j: (l, 0, j)),
            pl.BlockSpec((1, 1, tn), lambda l, j: (l, 0, j)),
        ],
        out_specs=pl.BlockSpec((1, 8, tn), lambda l, j: (l, 0, j)),
        out_shape=jax.ShapeDtypeStruct((DEPTH, 8, 3 * D_MODEL), F32),
        compiler_params=_cparams(("parallel", "parallel")),
        name="modulation",
    )(cond, w_mod, b_mod.reshape(DEPTH, 1, 3 * D_MODEL))


def _head_rms(x, bd, w):
    ss = _dot_exact_lhs_right(x * x, bd)
    return x * lax.rsqrt(ss * (1.0 / HD) + EPS) * w


def _dot_exact_lhs_right(x, m):
    hi, lo = _split(x)
    return _dot(hi, m) + _dot(lo, m)


def _rope(x, cos, sa, sb):
    return x * cos + pltpu.roll(x, LANE - ROT_AXIS // 2, axis=1) * sa + pltpu.roll(x, ROT_AXIS // 2, axis=1) * sb


def _short_conv_norm(zc, zp, zn, first, last, w_ref, xe_ref, o_ref):
    rows = zc.shape[0]
    xe_ref[0:HALO, :] = jnp.where(first, 0.0, zp)
    xe_ref[HALO:HALO + rows, :] = zc
    xe_ref[HALO + rows:, :] = jnp.where(last, 0.0, zn)
    qk_w = 2 * D_HEADS * D_DK

    def one_head(h):
        cols = slice(h * D_DK, (h + 1) * D_DK)
        acc = None
        for j in range(CONV_K):
            start = HALO - CONV_K // 2 + j
            term = xe_ref[start:start + rows, cols] * w_ref[j:j + 1, cols]
            acc = term if acc is None else acc + term
        y = _silu(acc)
        if h * D_DK < qk_w:
            y = y * lax.rsqrt(jnp.sum(y * y, axis=-1, keepdims=True) + EPS)
            if h < D_HEADS:
                y = y * (D_DK ** -0.5)
        o_ref[:, cols] = y

    return [functools.partial(one_head, h) for h in range(3 * D_HEADS)]


def _inproj_kernel(*refs, even, rope, tiles_per_batch):
    x_ref, shift_ref, scale_ref, g_ref, w_ref = refs[:5]
    pos = 5
    if even:
        qn_ref, kn_ref, bd_ref = refs[pos:pos + 3]
        pos += 3
    else:
        xp_ref, xn_ref, cw_ref = refs[pos:pos + 3]
        pos += 3
    if rope:
        cos_ref, sa_ref, sb_ref = refs[pos:pos + 3]
        pos += 3
    o_ref = refs[pos]
    if not even:
        xe_ref = refs[pos + 1]

    def modulated(x):
        y = x * lax.rsqrt(jnp.mean(x * x, axis=-1, keepdims=True) + EPS) * g_ref[...]
        return (y * (1.0 + scale_ref[0]) + shift_ref[0]).astype(BF16)

    h = modulated(x_ref[...])

    q_off, k_off = (E_Q, E_K) if even else (O_Q, O_K)
    width = E_W if even else O_W
    conv_w = 0 if even else 3 * D_HEADS * D_DK
    if not even:
        halo = modulated(jnp.concatenate([xp_ref[...], xn_ref[...]], axis=0))
        zc = jnp.dot(h, w_ref[:, 0:conv_w], preferred_element_type=F32)
        zh = jnp.dot(halo, w_ref[:, 0:conv_w], preferred_element_type=F32)

    def transform(z, is_q):
        if even:
            z = _head_rms(z, bd_ref[...], qn_ref[...] if is_q else kn_ref[...])
        if rope:
            z = _rope(z, cos_ref[...], sa_ref[...], sb_ref[...])
        return z

    wide = 2 * LANE
    pending = []
    spans = [(conv_w, width)]
    if even or rope:
        starts = [q_off + i * wide for i in range(QW // wide)] + [k_off]
        pending = [(off, jnp.dot(h, w_ref[:, off:off + wide], preferred_element_type=F32)) for off in starts]
        spans, cur = [], conv_w
        for off in sorted(starts):
            if off > cur:
                spans.append((cur, off))
            cur = off + wide
        if cur < width:
            spans.append((cur, width))
    def project(s, e):
        o_ref[:, s:e] = jnp.dot(h, w_ref[:, s:e], preferred_element_type=F32)

    step = wide
    dots = [functools.partial(project, s, min(s + step, b)) for a, b in spans for s in range(a, b, step)]
    vpu = []
    if not even:
        i = pl.program_id(0)
        first = (i % tiles_per_batch) == 0
        last = (i % tiles_per_batch) == tiles_per_batch - 1
        vpu = _short_conv_norm(zc, zh[:HALO], zh[HALO:], first, last, cw_ref, xe_ref, o_ref)
    for k in range(max(len(dots), len(vpu))):
        if k < len(vpu):
            vpu[k]()
        if k < len(dots):
            dots[k]()
    for off, z in pending:
        if off == k_off:
            o_ref[:, off:off + LANE] = transform(z[:, :LANE], False)
            o_ref[:, off + LANE:off + wide] = z[:, LANE:]
        else:
            for i in range(wide // LANE):
                o_ref[:, off + i * LANE:off + (i + 1) * LANE] = transform(z[:, i * LANE:(i + 1) * LANE], True)


def _inproj(x, shift, scale, g_pre, w, *, even, rope_tabs=None, qn=None, kn=None, bd=None, conv_w=None,
            rows_per_batch):
    T = x.shape[0]
    width = w.shape[1]
    tpb = rows_per_batch // ROW_TILE
    rope = rope_tabs is not None
    per_batch = shift.shape[0] > 1
    mod_idx = (lambda i: (i // tpb, 0, 0)) if per_batch else (lambda i: (0, 0, 0))
    ins = [x, shift, scale, g_pre.reshape(1, D_MODEL), w]
    specs = [
        pl.BlockSpec((ROW_TILE, D_MODEL), lambda i: (i, 0)),
        pl.BlockSpec((1, 1, D_MODEL), mod_idx),
        pl.BlockSpec((1, 1, D_MODEL), mod_idx),
        pl.BlockSpec((1, D_MODEL), lambda i: (0, 0)),
        pl.BlockSpec((D_MODEL, width), lambda i: (0, 0)),
    ]
    if even:
        ins += [qn, kn, bd]
        specs += [pl.BlockSpec((1, LANE), lambda i: (0, 0)),
                  pl.BlockSpec((1, LANE), lambda i: (0, 0)),
                  pl.BlockSpec((LANE, LANE), lambda i: (0, 0))]
        scratch = []
    else:
        hb = ROW_TILE // HALO
        n_hb = T // HALO
        cw = conv_w.shape[1]
        ins += [x, x, conv_w]
        specs += [pl.BlockSpec((HALO, D_MODEL), lambda i: (jnp.maximum(i * hb - 1, 0), 0)),
                  pl.BlockSpec((HALO, D_MODEL), lambda i: (jnp.minimum((i + 1) * hb, n_hb - 1), 0)),
                  pl.BlockSpec((CONV_K, cw), lambda i: (0, 0))]
        scratch = [pltpu.VMEM((ROW_TILE + 2 * HALO, cw), F32)]
    if rope:
        ins += list(rope_tabs)
        specs += [pl.BlockSpec((ROW_TILE, LANE), lambda i: (i % tpb, 0))] * 3
    return pl.pallas_call(
        functools.partial(_inproj_kernel, even=even, rope=rope, tiles_per_batch=tpb),
        grid=(T // ROW_TILE,),
        in_specs=specs,
        out_specs=pl.BlockSpec((ROW_TILE, width), lambda i: (i, 0)),
        out_shape=jax.ShapeDtypeStruct((T, width), F32),
        scratch_shapes=scratch,
        compiler_params=_cparams(("parallel",)),
        name="inproj_even" if even else "inproj_odd",
    )(*ins)


LOG2E = 1.4426950408889634
SCORE_SCALE = HD ** -0.5 * LOG2E


def _values_with_ones(v):
    lower = _iota(v.shape, 1) < HD
    return jnp.where(lower, v, 1.0).astype(BF16), jnp.where(lower, 1.0, v).astype(BF16)


def _pipelined_heads(scores, finish, lookahead, lower, g_ref, o_ref):
    heads = [(c, e) for c in range(QW // LANE) for e in range(N_KV)]
    queue = [scores(*h) for h in heads[:lookahead]]
    halves = {}
    for i, (c, e) in enumerate(heads):
        cur = queue.pop(0)
        if i + lookahead < len(heads):
            queue.append(scores(*heads[i + lookahead]))
        halves[e] = finish(c, e, *cur)
        if e == N_KV - 1:
            oc = jnp.where(lower, halves[0], halves[1])
            o_ref[:, c * LANE:(c + 1) * LANE] = oc * _silu(g_ref[:, c * LANE:(c + 1) * LANE])


def _attn_dense_kernel(*refs, has_ctx, has_sink, tq, lookahead):
    q_ref, k_ref, v_ref, g_ref = refs[:4]
    pos = 4
    if has_ctx:
        kc_ref, vc_ref = refs[pos:pos + 2]
        pos += 2
    if has_sink:
        sink_ref = refs[pos]
        pos += 1
    o_ref, kb_ref, vb0_ref, vb1_ref = refs[pos:pos + 4]

    @pl.when(pl.program_id(1) == 0)
    def _():
        kb_ref[...] = k_ref[...].astype(BF16)
        vb0_ref[...], vb1_ref[...] = _values_with_ones(v_ref[...])

    kb = kb_ref[...]
    vb = (vb0_ref[...], vb1_ref[...])
    if has_ctx:
        kcb = kc_ref[0].astype(BF16)
        vcb = _values_with_ones(vc_ref[0])
    lower = _iota((tq, LANE), 1) < HD

    def scores(c, e):
        qc = q_ref[:, c * LANE:(c + 1) * LANE] * SCORE_SCALE
        qe = jnp.where(lower if e == 0 else jnp.logical_not(lower), qc, 0.0).astype(BF16)
        return _dot_nt(qe, kb), (_dot_nt(qe, kcb) if has_ctx else None)

    def finish(c, e, s, sc):
        m = jnp.max(s, axis=-1, keepdims=True)
        if has_ctx:
            m = jnp.maximum(m, jnp.max(sc, axis=-1, keepdims=True))
        if has_sink:
            sk = sink_ref[c * N_KV + e] * LOG2E
            m = jnp.maximum(m, sk)
        o = _dot(jnp.exp2((s - m).astype(BF16)), vb[e])
        if has_ctx:
            o = o + _dot(jnp.exp2(sc - m), vcb[e])
        l = pltpu.roll(o, HD, axis=1)
        if has_sink:
            l = l + jnp.exp2(sk - m)
        return o / l

    _pipelined_heads(scores, finish, lookahead, lower, g_ref, o_ref)


def _attn_dense(z, *, n_batch, seq, tq, q_off, g_off, k_off, v_off, ctx=None, sink=None):
    T = z.shape[0]
    nq = seq // tq
    has_ctx = ctx is not None
    has_sink = sink is not None
    ins = [z, z, z, z]
    specs = [
        pl.BlockSpec((tq, QW), lambda b, i: (b * nq + i, q_off // QW)),
        pl.BlockSpec((seq, KVW), lambda b, i: (b, k_off // KVW)),
        pl.BlockSpec((seq, KVW), lambda b, i: (b, v_off // KVW)),
        pl.BlockSpec((tq, QW), lambda b, i: (b * nq + i, g_off // QW)),
    ]
    if has_ctx:
        ins += list(ctx)
        n_ctx = ctx[0].shape[1]
        specs += [pl.BlockSpec((1, n_ctx, KVW), lambda b, i: (b, 0, 0))] * 2
    if has_sink:
        ins.append(sink)
        specs.append(pl.BlockSpec(memory_space=pltpu.SMEM))
    return pl.pallas_call(
        functools.partial(_attn_dense_kernel, has_ctx=has_ctx, has_sink=has_sink, tq=tq,
                          lookahead=N_QH if seq <= 1024 else 1),
        grid=(n_batch, nq),
        in_specs=specs,
        out_specs=pl.BlockSpec((tq, QW), lambda b, i: (b * nq + i, 0)),
        out_shape=jax.ShapeDtypeStruct((T, QW), F32),
        scratch_shapes=[pltpu.VMEM((seq, KVW), BF16)] * 3,
        compiler_params=_cparams(("arbitrary", "arbitrary")),
        name="attn_dense",
    )(*ins)


def _attn_band_kernel(q_ref, kp_ref, kc_ref, kn_ref, vp_ref, vc_ref, vn_ref, g_ref, kx_ref, vx_ref, sink_ref,
                      o_ref, *, seq):
    n = pl.program_id(1)
    tq = Q_BLOCK
    kcat = jnp.concatenate([kp_ref[...], kc_ref[...], kn_ref[...]], axis=0).astype(BF16)
    vcat = _values_with_ones(jnp.concatenate([vp_ref[...], vc_ref[...], vn_ref[...]], axis=0))
    kxb = kx_ref[0].astype(BF16)
    vxb = _values_with_ones(vx_ref[0])
    qpos = n * tq + _iota((tq, 3 * tq), 0)
    kpos = (n - 1) * tq + _iota((tq, 3 * tq), 1)
    valid = (jnp.abs(qpos - kpos) <= WINDOW) & (kpos >= 0) & (kpos < seq)
    lower = _iota((tq, LANE), 1) < HD

    def scores(c, e):
        qc = q_ref[:, c * LANE:(c + 1) * LANE] * SCORE_SCALE
        qe = jnp.where(lower if e == 0 else jnp.logical_not(lower), qc, 0.0).astype(BF16)
        return _dot_nt(qe, kcat), _dot_nt(qe, kxb)

    def finish(c, e, s, sx):
        s = jnp.where(valid, s, -jnp.inf)
        sk = sink_ref[c * N_KV + e] * LOG2E
        m = jnp.maximum(jnp.maximum(jnp.max(s, axis=-1, keepdims=True),
                                    jnp.max(sx, axis=-1, keepdims=True)), sk)
        o = _dot(jnp.exp2(s - m), vcat[e]) + _dot(jnp.exp2(sx - m), vxb[e])
        l = pltpu.roll(o, HD, axis=1) + jnp.exp2(sk - m)
        return o / l

    _pipelined_heads(scores, finish, N_QH, lower, g_ref, o_ref)


def _attn_band(z, ctx, sink, *, n_batch, seq):
    T = z.shape[0]
    nq = seq // Q_BLOCK
    n_ctx = ctx[0].shape[1]

    def kv_spec(off, delta):
        return pl.BlockSpec(
            (Q_BLOCK, KVW),
            lambda b, i: (b * nq + jnp.clip(i + delta, 0, nq - 1), off // KVW))

    specs = [
        pl.BlockSpec((Q_BLOCK, QW), lambda b, i: (b * nq + i, O_Q // QW)),
        kv_spec(O_K, -1), kv_spec(O_K, 0), kv_spec(O_K, 1),
        kv_spec(O_V, -1), kv_spec(O_V, 0), kv_spec(O_V, 1),
        pl.BlockSpec((Q_BLOCK, QW), lambda b, i: (b * nq + i, O_G // QW)),
        pl.BlockSpec((1, n_ctx, KVW), lambda b, i: (b, 0, 0)),
        pl.BlockSpec((1, n_ctx, KVW), lambda b, i: (b, 0, 0)),
        pl.BlockSpec(memory_space=pltpu.SMEM),
    ]
    return pl.pallas_call(
        functools.partial(_attn_band_kernel, seq=seq),
        grid=(n_batch, nq),
        in_specs=specs,
        out_specs=pl.BlockSpec((Q_BLOCK, QW), lambda b, i: (b * nq + i, 0)),
        out_shape=jax.ShapeDtypeStruct((T, QW), F32),
        compiler_params=_cparams(("parallel", "parallel")),
        name="attn_band",
    )(z, z, z, z, z, z, z, z, ctx[0], ctx[1], sink)


def _gla_tile(dirs, n_chunk):
    rows = n_chunk * CHUNK
    shift = CHUNK.bit_length() - 1
    ii = _iota((rows, rows), 0)
    jj = _iota((rows, rows), 1)
    same = lax.shift_right_logical(ii, shift) == lax.shift_right_logical(jj, shift)
    sel = jnp.where(lax.shift_right_logical(_iota((rows, n_chunk * B_DV), 0), shift)
                    == lax.shift_right_logical(_iota((rows, n_chunk * B_DV), 1), B_DV.bit_length() - 1),
                    1.0, 0.0).astype(BF16)
    ci = _iota((CHUNK, CHUNK), 0)
    cj = _iota((CHUNK, CHUNK), 1)
    lower = _iota((CHUNK, LANE), 1) < B_DK
    n_pair = B_HEADS * B_DK // LANE
    per_pair = LANE // B_DK

    gs = []
    for (_, _, _, r_ref, up, bias, _, _, _) in dirs:
        gk = _dot(r_ref[...], up) + bias
        gs.append(-_softplus(-gk) * (1.0 / GLA_NORMALIZER))
    bs = []
    for g, d in zip(gs, dirs):
        tri = jnp.where(same & ((jj >= ii) if d[8] else (jj <= ii)), 1.0, 0.0).astype(BF16)
        bs.append(_dot_exact_lhs(tri, g))
    decargs = [_dot_tn_exact_rhs(g, sel) for g in gs]

    units = []
    for di, (q_ref, k_ref, v_ref, _, _, _, st_ref, o_ref, rev) in enumerate(dirs):
        b = bs[di]
        incl = (cj >= ci) if rev else (cj <= ci)
        order = range(n_chunk - 1, -1, -1) if rev else range(n_chunk)
        for c in order:
            r0 = c * CHUNK
            bc = b[r0:r0 + CHUNK]
            btot = bc[0:1, :] if rev else bc[CHUNK - 1:CHUNK, :]
            qt = q_ref[r0:r0 + CHUNK, :] * (B_DK ** -0.5) * jnp.exp(bc)
            kc = k_ref[r0:r0 + CHUNK, :]
            kt = kc * jnp.exp(-bc)
            kend = kc * jnp.exp(btot - bc)
            for p in range(n_pair):
                for e in range(per_pair):
                    h = p * per_pair + e
                    qm = jnp.where(lower if e == 0 else jnp.logical_not(lower), qt[:, p * LANE:(p + 1) * LANE], 0.0)
                    units.append(dict(di=di, c=c, p=p, e=e, incl=incl, qm=qm.astype(BF16),
                                      kt=kt[:, p * LANE:(p + 1) * LANE].astype(BF16),
                                      kend=kend[:, p * LANE:(p + 1) * LANE].astype(BF16),
                                      v=v_ref[r0:r0 + CHUNK, h * B_DV:(h + 1) * B_DV].astype(BF16)))
    for u in units:
        u["att"] = jnp.where(u["incl"], _dot_nt(u["qm"], u["kt"]), 0.0).astype(BF16)
    for u in units:
        u["upd"] = _dot_tn(u["kend"], u["v"])[u["e"] * B_DK:(u["e"] + 1) * B_DK]
    for di, d in enumerate(dirs):
        st = d[6][...]
        mine = [u for u in units if u["di"] == di]
        for k0 in range(0, len(mine), B_HEADS):
            grp = mine[k0:k0 + B_HEADS]
            c = grp[0]["c"]
            for u in grp:
                u["st"] = st[u["p"] * LANE:(u["p"] + 1) * LANE].astype(BF16)
            dec = jnp.exp(decargs[di][:, c * B_DV:(c + 1) * B_DV])
            st = dec * st + jnp.concatenate([u["upd"] for u in grp], axis=0)
        d[6][...] = st
    for u in units:
        u["qs"] = _dot(u["qm"], u["st"])
    for u in units:
        u["o"] = _dot(u["att"], u["v"]) + u["qs"]
    for di, d in enumerate(dirs):
        mine = [u for u in units if u["di"] == di]
        for k0 in range(0, len(mine), B_HEADS):
            grp = mine[k0:k0 + B_HEADS]
            r0 = grp[0]["c"] * CHUNK
            d[7][r0:r0 + CHUNK, :] = jnp.concatenate([u["o"] for u in grp], axis=1)


def _gla_kernel(*refs, has_s0, n_chunk):
    qf, kf, vf, rf, qb, kb, vb, rb, up_ref, bias_ref = refs[:10]
    pos = 10
    if has_s0:
        s0f, s0b = refs[pos:pos + 2]
        pos += 2
    of_ref, ob_ref, sf_ref, sb_ref, stf, stb = refs[pos:pos + 6]
    n = pl.program_id(1)

    @pl.when(n == 0)
    def _():
        if has_s0:
            stf[...] = s0f[0]
            stb[...] = s0b[0]
        else:
            stf[...] = jnp.zeros_like(stf)
            stb[...] = jnp.zeros_like(stb)

    _gla_tile([(qf, kf, vf, rf, up_ref[0], bias_ref[0], stf, of_ref, False),
               (qb, kb, vb, rb, up_ref[1], bias_ref[1], stb, ob_ref, True)], n_chunk)

    @pl.when(n == pl.num_programs(1) - 1)
    def _():
        sf_ref[0] = stf[...]
        sb_ref[0] = stb[...]


def _gla(z, up_ext, bias, s0, *, n_batch, seq):
    T = z.shape[0]
    nt = seq // REC_TILE
    has_s0 = s0 is not None
    sw = B_HEADS * B_DK

    def fwd(b, i):
        return b * nt + i

    def bwd(b, i):
        return b * nt + nt - 1 - i

    def specs_for(row):
        return [
            pl.BlockSpec((REC_TILE, sw), lambda b, i: (row(b, i), E_BQ // sw)),
            pl.BlockSpec((REC_TILE, sw), lambda b, i: (row(b, i), E_BK // sw)),
            pl.BlockSpec((REC_TILE, QW), lambda b, i: (row(b, i), E_BV // QW)),
            pl.BlockSpec((REC_TILE, LANE), lambda b, i: (row(b, i), E_R // LANE)),
        ]

    ins = [z] * 8 + [up_ext, bias]
    specs = specs_for(fwd) + specs_for(bwd) + [
        pl.BlockSpec((2, LANE, sw), lambda b, i: (0, 0, 0)),
        pl.BlockSpec((2, 1, sw), lambda b, i: (0, 0, 0)),
    ]
    if has_s0:
        ins += [s0[0], s0[1]]
        specs += [pl.BlockSpec((1, sw, B_DV), lambda b, i: (b, 0, 0))] * 2
    st_shape = jax.ShapeDtypeStruct((n_batch, sw, B_DV), F32)
    o_shape = jax.ShapeDtypeStruct((T, QW), F32)
    return pl.pallas_call(
        functools.partial(_gla_kernel, has_s0=has_s0, n_chunk=REC_TILE // CHUNK),
        grid=(n_batch, nt),
        in_specs=specs,
        out_specs=[
            pl.BlockSpec((REC_TILE, QW), lambda b, i: (fwd(b, i), 0)),
            pl.BlockSpec((REC_TILE, QW), lambda b, i: (bwd(b, i), 0)),
            pl.BlockSpec((1, sw, B_DV), lambda b, i: (b, 0, 0)),
            pl.BlockSpec((1, sw, B_DV), lambda b, i: (b, 0, 0)),
        ],
        out_shape=[o_shape, o_shape, st_shape, st_shape],
        scratch_shapes=[pltpu.VMEM((sw, B_DV), F32), pltpu.VMEM((sw, B_DV), F32)],
        compiler_params=_cparams(("arbitrary", "arbitrary")),
        name="gla",
    )(*ins)


def _dot3_stacked(lhs_list, rhs, expand=None):
    rh, rl = _split(rhs)
    if expand is not None:
        rh, rl = expand(rh), expand(rl)
    parts = [_split(x) for x in lhs_list]
    his = [p[0] for p in parts]
    los = [p[1] for p in parts]
    n = lhs_list[0].shape[0]
    k = len(lhs_list)
    r1 = _dot(jnp.concatenate(his + los, axis=0), rh)
    r2 = _dot(jnp.concatenate(his, axis=0) if k > 1 else his[0], rl)
    return [r1[i * n:(i + 1) * n] + (r1[(k + i) * n:(k + i + 1) * n] + r2[i * n:(i + 1) * n]) for i in range(k)]


def _delta_tile(dirs, alog, dtb, n_chunk):
    C = CHUNK
    W = D_HEADS * C
    hk = D_HEADS * D_DK
    sh = C.bit_length() - 1
    ii = _iota((C, C), 0)
    jj = _iota((C, C), 1)
    pi = _iota((C, W), 0)
    pl_ = _iota((C, W), 1)
    pj = pl_ & (C - 1)
    pblk = lax.shift_right_logical(pl_, sh)
    eye_p = jnp.where(pi == pj, 1.0, 0.0)
    bdm = (lax.shift_right_logical(_iota((W, W), 0), sh) == lax.shift_right_logical(_iota((W, W), 1), sh))
    bdk = (lax.shift_right_logical(_iota((W, hk), 0), sh)
           == lax.shift_right_logical(_iota((W, hk), 1), D_DK.bit_length() - 1))
    ones_cc = jnp.ones((C, C), BF16)

    def pack(cols):
        out = jnp.broadcast_to(cols[D_HEADS - 1], (C, W))
        for h in range(D_HEADS - 2, -1, -1):
            out = jnp.where(pblk == h, jnp.broadcast_to(cols[h], (C, W)), out)
        return out

    def wide(cols):
        return jnp.concatenate([jnp.broadcast_to(c, (C, D_DK)) for c in cols], axis=1)

    def block_diag(x):
        return jnp.where(bdm, jnp.concatenate([x] * D_HEADS, axis=0), jnp.zeros((), x.dtype))

    units = []
    for d, (x_ref, ab_ref, st_ref, o_ref, rev) in enumerate(dirs):
        incl = (jj >= ii) if rev else (jj <= ii)
        tri = jnp.where(incl, 1.0, 0.0).astype(BF16)
        incl_p = (pj >= pi) if rev else (pj <= pi)
        strict_p = (pj > pi) if rev else (pj < pi)
        before_p = (pj <= pi) if rev else (pj >= pi)
        order = range(n_chunk - 1, -1, -1) if rev else range(n_chunk)
        for c in order:
            r0 = c * C
            ab = ab_ref[r0:r0 + C, :]
            gall = -jnp.exp(alog) * _softplus(ab + dtb)
            beta_all = _sigmoid(ab)
            gc_all = _dot_exact_lhs(tri, gall)
            ig = [d * D_HEADS + h for h in range(D_HEADS)]
            gcols = [gc_all[:, i:i + 1] for i in ig]
            g_p = pack([gall[:, i:i + 1] for i in ig])
            gc_p = pack(gcols)
            gr_p = _dot_exact_lhs(ones_cc, jnp.where(before_p, g_p, 0.0))
            decay = jnp.exp(jnp.where(incl_p, gc_p - gr_p, -jnp.inf))
            gc_w = wide(gcols)
            beta_w = wide([beta_all[:, 2 * D_HEADS + i:2 * D_HEADS + i + 1] for i in ig])
            glast_w = gc_w[0:1, :] if rev else gc_w[C - 1:C, :]
            egc_w = jnp.exp(gc_w)
            q_all = x_ref[r0:r0 + C, 0:hk]
            k_all = x_ref[r0:r0 + C, hk:2 * hk]
            v_all = x_ref[r0:r0 + C, 2 * hk:3 * hk]
            kb_all = k_all * beta_w
            k_bd = jnp.where(bdk, jnp.concatenate([k_all] * D_HEADS, axis=0), 0.0)
            r = _dot_nt(jnp.concatenate([kb_all, q_all], axis=0), k_bd)
            p = -jnp.where(strict_p, r[:C] * decay, 0.0)
            vb_all = v_all * beta_w
            kbe_all = kb_all * egc_w
            rhs = jnp.concatenate(
                [jnp.concatenate([vb_all[:, h * D_DV:(h + 1) * D_DV], kbe_all[:, h * D_DK:(h + 1) * D_DK]], axis=1)
                 for h in range(D_HEADS)], axis=0)
            units.append(dict(d=d, r0=r0, st_ref=st_ref, o_ref=o_ref, p=p, acc=eye_p + p, att=r[C:] * decay,
                              rhs=rhs, qdec=q_all * egc_w, kdec=k_all * jnp.exp(glast_w - gc_w),
                              eglast=jnp.exp(glast_w)))
    per_pair = LANE // C
    half = lax.shift_right_logical(_iota((C, LANE), 1), sh)

    def square_first(us):
        for u in us:
            u["p"] = _dot3_stacked([u["p"]], u["p"], block_diag)[0]

    def square_and_extend(us):
        for u in us:
            pp, ap = _dot3_stacked([u["p"], u["acc"]], u["p"], block_diag)
            u["acc"] = u["acc"] + ap
            u["p"] = pp

    def extend_last(us):
        for u in us:
            u["acc"] = u["acc"] + _dot3_stacked([u["acc"]], u["p"], block_diag)[0]

    def solve(us):
        for u in us:
            u["sol"] = []
            for c in range(D_HEADS // per_pair):
                lanes = slice(c * LANE, (c + 1) * LANE)
                lhs = [jnp.where(half == e, u["acc"][:, lanes], 0.0) for e in range(per_pair)]
                u["sol"] += _dot3_stacked(lhs, u["rhs"][c * LANE:(c + 1) * LANE])

    n_mid = 0
    m = 2
    while 2 * m < C:
        n_mid += 1
        m *= 2
    local_stages = [square_first] + [square_and_extend] * n_mid + [extend_last, solve]

    def read_state(cur):
        for u in cur:
            u["s"] = [u["st_ref"][h] for h in range(D_HEADS)]
            u["vnew"], u["qs"] = [], []
            for h in range(D_HEADS):
                lhs = jnp.concatenate([u["sol"][h][:, D_DV:], u["qdec"][:, h * D_DK:(h + 1) * D_DK]], axis=0)
                r = _dot(lhs, u["s"][h])
                u["vnew"].append(u["sol"][h][:, :D_DV] - r[:C])
                u["qs"].append(r[C:])

    def write_state(cur):
        for u in cur:
            o_pairs = []
            for c in range(D_HEADS // per_pair):
                lanes = slice(c * LANE, (c + 1) * LANE)
                att_rows = jnp.concatenate(
                    [jnp.where(half == e, u["att"][:, lanes], 0.0) for e in range(per_pair)],
                    axis=0)
                o_pairs.append(_dot(att_rows, jnp.concatenate(u["vnew"][c * per_pair:(c + 1) * per_pair], axis=0)))
            o_all = jnp.concatenate(o_pairs, axis=0)
            outs = []
            for h in range(D_HEADS):
                outs.append(u["qs"][h] + o_all[h * C:(h + 1) * C])
                u["st_ref"][h] = (u["s"][h] * u["eglast"][:, h * D_DK:(h + 1) * D_DK]
                                  + _dot_tn(u["kdec"][:, h * D_DK:(h + 1) * D_DK], u["vnew"][h]))
            u["o_ref"][u["r0"]:u["r0"] + C, :] = jnp.concatenate(outs, axis=1)

    per_dir = [[u for u in units if u["d"] == d] for d in range(len(dirs))]
    steps = [[lst[i] for lst in per_dir] for i in range(n_chunk)]
    n_first = n_chunk // 2
    for stage in local_stages:
        stage([u for cur in steps[:n_first] for u in cur])
    pending = [functools.partial(f, cur) for cur in steps[:n_first] for f in (read_state, write_state)]
    for stage in local_stages:
        stage([u for cur in steps[n_first:] for u in cur])
        if pending:
            pending.pop(0)()
    for f in pending:
        f()
    for cur in steps[n_first:]:
        read_state(cur)
        write_state(cur)


def _delta_kernel(*refs, has_s0, n_chunk):
    xf, af, xb, ab_, alog_ref, dtb_ref = refs[:6]
    pos = 6
    if has_s0:
        s0f, s0b = refs[pos:pos + 2]
        pos += 2
    of_ref, ob_ref, sf_ref, sb_ref, stf, stb = refs[pos:pos + 6]
    n = pl.program_id(1)

    @pl.when(n == 0)
    def _():
        if has_s0:
            stf[...] = s0f[0]
            stb[...] = s0b[0]
        else:
            stf[...] = jnp.zeros_like(stf)
            stb[...] = jnp.zeros_like(stb)

    _delta_tile([(xf, af, stf, of_ref, False), (xb, ab_, stb, ob_ref, True)],
                alog_ref[...], dtb_ref[...], n_chunk)

    @pl.when(n == pl.num_programs(1) - 1)
    def _():
        sf_ref[0] = stf[...]
        sb_ref[0] = stb[...]


def _delta(z, alog_row, dtb_row, s0, *, n_batch, seq):
    T = z.shape[0]
    nt = seq // REC_TILE
    has_s0 = s0 is not None
    cw = 3 * D_HEADS * D_DK
    assert O_DQKV == 0 and ROW_TILE == REC_TILE

    def fwd(b, i):
        return b * nt + i

    def bwd(b, i):
        return b * nt + nt - 1 - i

    ins = [z, z, z, z, alog_row, dtb_row]
    specs = [
        pl.BlockSpec((REC_TILE, cw), lambda b, i: (fwd(b, i), 0)),
        pl.BlockSpec((REC_TILE, LANE), lambda b, i: (fwd(b, i), O_AB // LANE)),
        pl.BlockSpec((REC_TILE, cw), lambda b, i: (bwd(b, i), 0)),
        pl.BlockSpec((REC_TILE, LANE), lambda b, i: (bwd(b, i), O_AB // LANE)),
        pl.BlockSpec((1, LANE), lambda b, i: (0, 0)),
        pl.BlockSpec((1, LANE), lambda b, i: (0, 0)),
    ]
    st_block = (1, D_HEADS, D_DK, D_DV)
    if has_s0:
        ins += [s0[0], s0[1]]
        specs += [pl.BlockSpec(st_block, lambda b, i: (b, 0, 0, 0))] * 2
    st_shape = jax.ShapeDtypeStruct((n_batch, D_HEADS, D_DK, D_DV), F32)
    o_shape = jax.ShapeDtypeStruct((T, QW), F32)
    return pl.pallas_call(
        functools.partial(_delta_kernel, has_s0=has_s0, n_chunk=REC_TILE // CHUNK),
        grid=(n_batch, nt),
        in_specs=specs,
        out_specs=[
            pl.BlockSpec((REC_TILE, QW), lambda b, i: (fwd(b, i), 0)),
            pl.BlockSpec((REC_TILE, QW), lambda b, i: (bwd(b, i), 0)),
            pl.BlockSpec(st_block, lambda b, i: (b, 0, 0, 0)),
            pl.BlockSpec(st_block, lambda b, i: (b, 0, 0, 0)),
        ],
        out_shape=[o_shape, o_shape, st_shape, st_shape],
        scratch_shapes=[pltpu.VMEM((D_HEADS, D_DK, D_DV), F32), pltpu.VMEM((D_HEADS, D_DK, D_DV), F32)],
        compiler_params=_cparams(("arbitrary", "arbitrary")),
        name="delta",
    )(*ins)


def _outproj_kernel(ya_ref, of_ref, ob_ref, zg_ref, on_ref, w_ref, x_ref, gate_ref, gp_ref, o_ref):
    ob = of_ref[...] + ob_ref[...]
    zg = zg_ref[...]
    on = on_ref[...]
    parts = []
    for h in range(QW // LANE):
        seg = ob[:, h * LANE:(h + 1) * LANE]
        nrm = seg * lax.rsqrt(jnp.mean(seg * seg, axis=-1, keepdims=True) + EPS) * on
        parts.append(nrm * _silu(zg[:, h * LANE:(h + 1) * LANE]))
    yb = jnp.concatenate(parts, axis=1)
    out = _dot(ya_ref[...], w_ref[0:QW, :]) + _dot(yb, w_ref[QW:, :])
    post = out * lax.rsqrt(jnp.mean(out * out, axis=-1, keepdims=True) + EPS) * gp_ref[...]
    o_ref[...] = x_ref[...] + gate_ref[0] * post


def _outproj(ya, o_f, o_b, z, g_off, onorm, w_out, x, gate, g_post, *, rows_per_batch):
    T = x.shape[0]
    tpb = rows_per_batch // OUT_TILE
    row = lambda i: (i, 0)
    per_batch = gate.shape[0] > 1
    mod_idx = (lambda i: (i // tpb, 0, 0)) if per_batch else (lambda i: (0, 0, 0))
    return pl.pallas_call(
        _outproj_kernel,
        grid=(T // OUT_TILE,),
        in_specs=[
            pl.BlockSpec((OUT_TILE, QW), row),
            pl.BlockSpec((OUT_TILE, QW), row),
            pl.BlockSpec((OUT_TILE, QW), row),
            pl.BlockSpec((OUT_TILE, QW), lambda i: (i, g_off // QW)),
            pl.BlockSpec((1, LANE), lambda i: (0, 0)),
            pl.BlockSpec((2 * QW, D_MODEL), lambda i: (0, 0)),
            pl.BlockSpec((OUT_TILE, D_MODEL), row),
            pl.BlockSpec((1, 1, D_MODEL), mod_idx),
            pl.BlockSpec((1, D_MODEL), lambda i: (0, 0)),
        ],
        out_specs=pl.BlockSpec((OUT_TILE, D_MODEL), row),
        out_shape=jax.ShapeDtypeStruct((T, D_MODEL), F32),
        compiler_params=_cparams(("parallel",)),
        name="outproj",
    )(ya, o_f, o_b, z, onorm.reshape(1, LANE), w_out, x, gate, g_post.reshape(1, D_MODEL))


def _rope_tables(seq):
    n_rows = seq // GRID_W
    rows = jnp.repeat(jnp.arange(n_rows, dtype=F32), GRID_W)
    cols = jnp.tile(jnp.arange(GRID_W, dtype=F32), n_rows)
    inv = jnp.power(ROPE_THETA, jnp.arange(ROT_AXIS // 2, dtype=F32) * (-2.0 / ROT_AXIS))
    ang_r = rows[:, None] * inv[None, :]
    ang_c = cols[:, None] * inv[None, :]
    zero = jnp.zeros_like(ang_r)
    cos = jnp.concatenate([jnp.cos(ang_r)] * 2 + [jnp.cos(ang_c)] * 2, axis=1)
    sa = jnp.concatenate([-jnp.sin(ang_r), zero, -jnp.sin(ang_c), zero], axis=1)
    sb = jnp.concatenate([zero, jnp.sin(ang_r), zero, jnp.sin(ang_c)], axis=1)
    return tuple(jnp.tile(t, (1, LANE // HD)) for t in (cos, sa, sb))


def _regroup_cols(w):
    g = N_QH // N_KV
    lead = w.shape[:-1]
    return jnp.swapaxes(w.reshape(lead + (N_KV, g, HD)), -3, -2).reshape(lead + (QW,))


def _regroup_rows(w):
    g = N_QH // N_KV
    return jnp.swapaxes(w.reshape(w.shape[:-2] + (N_KV, g, HD, w.shape[-1])), -4, -3).reshape(w.shape)


def _prep_even_weights(w_in, w_out, gk_up):
    n = w_in.shape[0]
    w_in = w_in.astype(BF16)
    a_q, a_k, a_v, a_g, b_q, b_k, b_v, b_r, b_g = jnp.split(
        w_in, [512, 640, 768, 1280, 1536, 1792, 2304, 2336], axis=2)
    pad = jnp.zeros((n, D_MODEL, LANE - 2 * B_RANK), BF16)
    w = jnp.concatenate([_regroup_cols(a_q), _regroup_cols(a_g), b_v, b_g, b_q, b_k, a_k, a_v, b_r, pad], axis=2)
    w_out = w_out.astype(BF16)
    wo = jnp.concatenate([_regroup_rows(w_out[:, :QW]), w_out[:, QW:]], axis=1)
    zeros = jnp.zeros((n, B_RANK, B_HEADS * B_DK), BF16)
    tail = jnp.zeros((n, LANE - 2 * B_RANK, B_HEADS * B_DK), BF16)
    gk_up = gk_up.astype(BF16)
    up = jnp.stack([jnp.concatenate([gk_up[:, 0], zeros, tail], axis=1),
                    jnp.concatenate([zeros, gk_up[:, 1], tail], axis=1)], axis=1)
    return w, wo, up


def _prep_odd_weights(w_in, w_out):
    n = w_in.shape[0]
    w_in = w_in.astype(BF16)
    c_q, c_k, c_v, c_g, d_q, d_k, d_v, d_a, d_b, d_g = jnp.split(
        w_in, [512, 640, 768, 1280, 1792, 2304, 2816, 2824, 2832], axis=2)
    pad = jnp.zeros((n, D_MODEL, LANE - 4 * D_HEADS), BF16)
    w = jnp.concatenate([d_q, d_k, d_v, _regroup_cols(c_q), _regroup_cols(c_g), d_g, c_k, c_v, d_a, d_b, pad],
                        axis=2)
    w_out = w_out.astype(BF16)
    wo = jnp.concatenate([_regroup_rows(w_out[:, :QW]), w_out[:, QW:]], axis=1)
    return w, wo


def _lane_row(x):
    flat = x.reshape(-1).astype(F32)
    return jnp.zeros((1, LANE), F32).at[0, :flat.shape[0]].set(flat)


def kernel(x_prompt, x_sample, c, cache_attn_k, cache_attn_v, state_gla, cache_swa_k, cache_swa_v, state_delta,
           c_ctx, w_mod, b_mod, g_pre, g_post, w_in_even, w_out_even, qnorm_a, knorm_a, gla_gk_up, gla_gk_bias,
           gla_onorm, w_in_odd, w_out_odd, sink_c, conv_d, a_log_d, dt_bias_d, delta_onorm):
    n_p, seq_p, _ = x_prompt.shape
    n_s, seq_s, _ = x_sample.shape
    n_ctx = cache_attn_k.shape[2]
    assert (n_p * seq_p) % OUT_TILE == 0 and seq_s % OUT_TILE == 0 and seq_p % ROW_TILE == 0
    assert seq_s % ROW_TILE == 0 and seq_p % REC_TILE == 0 and seq_s % REC_TILE == 0
    assert n_s + 1 <= 8

    cond = jnp.zeros((8, D_MODEL), F32).at[0].set(c_ctx).at[1:1 + n_s].set(c)
    mod = _modulation(cond, w_mod, b_mod)
    rope = _rope_tables(seq_s)
    bd = jnp.asarray(np.kron(np.eye(LANE // HD), np.ones((HD, HD))), BF16)
    w_even, wo_even, up_even = _prep_even_weights(w_in_even, w_out_even, gla_gk_up)
    w_odd, wo_odd = _prep_odd_weights(w_in_odd, w_out_odd)
    sinks = jnp.swapaxes(sink_c.astype(F32).reshape(-1, N_KV, N_QH // N_KV), 1, 2).reshape(-1, N_QH)

    y_p = x_prompt.reshape(n_p * seq_p, D_MODEL)
    y_s = x_sample.reshape(n_s * seq_s, D_MODEL)
    attn_k, attn_v, gla_s, swa_k, swa_v, delta_s = [], [], [], [], [], []
    for l in range(DEPTH):
        i = l // 2
        shift, scale, gate = [mod[l, :, j * D_MODEL:(j + 1) * D_MODEL].reshape(8, 1, D_MODEL) for j in range(3)]
        mods_p = (shift[0:1], scale[0:1], gate[0:1])
        mods_s = (shift[1:1 + n_s], scale[1:1 + n_s], gate[1:1 + n_s])
        if l % 2 == 0:
            w, wo, up = w_even[i], wo_even[i], up_even[i]
            qn = jnp.tile(qnorm_a[i], LANE // HD).reshape(1, LANE)
            kn = jnp.tile(knorm_a[i], LANE // HD).reshape(1, LANE)
            bias = gla_gk_bias[i].reshape(2, 1, B_HEADS * B_DK)
            common = dict(even=True, qn=qn, kn=kn, bd=bd)
            z_p = _inproj(y_p, mods_p[0], mods_p[1], g_pre[l], w, rows_per_batch=seq_p, **common)
            z_s = _inproj(y_s, mods_s[0], mods_s[1], g_pre[l], w, rows_per_batch=seq_s, rope_tabs=rope, **common)
            offs = dict(q_off=E_Q, g_off=E_G, k_off=E_K, v_off=E_V)
            ya_p = _attn_dense(z_p, n_batch=n_p, seq=seq_p, tq=seq_p, **offs)
            ctx = (cache_attn_k[:, i].reshape(n_s, n_ctx, KVW), cache_attn_v[:, i].reshape(n_s, n_ctx, KVW))
            ya_s = _attn_dense(z_s, n_batch=n_s, seq=seq_s, tq=256, ctx=ctx, **offs)
            of_p, ob_p, sf_p, sb_p = _gla(z_p, up, bias, None, n_batch=n_p, seq=seq_p)
            s0 = state_gla[:, i].reshape(n_s, 2, B_HEADS * B_DK, B_DV)
            of_s, ob_s, _, _ = _gla(z_s, up, bias, (s0[:, 0], s0[:, 1]), n_batch=n_s, seq=seq_s)
            attn_k.append(z_p[:, E_K:E_K + KVW].reshape(n_p, seq_p, N_KV, HD))
            attn_v.append(z_p[:, E_V:E_V + KVW].reshape(n_p, seq_p, N_KV, HD))
            gla_s.append(jnp.stack([sf_p, sb_p], axis=1).reshape(n_p, 2, B_HEADS, B_DK, B_DV))
            g_off, onorm = E_BG, gla_onorm[i]
        else:
            w, wo = w_odd[i], wo_odd[i]
            z_p = _inproj(y_p, mods_p[0], mods_p[1], g_pre[l], w, even=False, conv_w=conv_d[i],
                          rows_per_batch=seq_p)
            z_s = _inproj(y_s, mods_s[0], mods_s[1], g_pre[l], w, even=False, conv_w=conv_d[i],
                          rows_per_batch=seq_s, rope_tabs=rope)
            sink = sinks[i]
            ya_p = _attn_dense(z_p, n_batch=n_p, seq=seq_p, tq=seq_p, q_off=O_Q, g_off=O_G, k_off=O_K,
                               v_off=O_V, sink=sink)
            ctx = (cache_swa_k[:, i].reshape(n_s, n_ctx, KVW), cache_swa_v[:, i].reshape(n_s, n_ctx, KVW))
            ya_s = _attn_band(z_s, ctx, sink, n_batch=n_s, seq=seq_s)
            alog_row = _lane_row(a_log_d[i])
            dtb_row = _lane_row(dt_bias_d[i])
            of_p, ob_p, sf_p, sb_p = _delta(z_p, alog_row, dtb_row, None, n_batch=n_p, seq=seq_p)
            s0 = state_delta[:, i]
            of_s, ob_s, _, _ = _delta(z_s, alog_row, dtb_row, (s0[:, 0], s0[:, 1]), n_batch=n_s, seq=seq_s)
            swa_k.append(z_p[:, O_K:O_K + KVW].reshape(n_p, seq_p, N_KV, HD))
            swa_v.append(z_p[:, O_V:O_V + KVW].reshape(n_p, seq_p, N_KV, HD))
            delta_s.append(jnp.stack([sf_p, sb_p], axis=1))
            g_off, onorm = O_DG, delta_onorm[i]
        y_p = _outproj(ya_p, of_p, ob_p, z_p, g_off, onorm, wo, y_p, mods_p[2], g_post[l], rows_per_batch=seq_p)
        y_s = _outproj(ya_s, of_s, ob_s, z_s, g_off, onorm, wo, y_s, mods_s[2], g_post[l], rows_per_batch=seq_s)
    return (y_p.reshape(n_p, seq_p, D_MODEL), y_s.reshape(n_s, seq_s, D_MODEL),
            jnp.stack(attn_k, axis=1), jnp.stack(attn_v, axis=1), jnp.stack(gla_s, axis=1),
            jnp.stack(swa_k, axis=1), jnp.stack(swa_v, axis=1), jnp.stack(delta_s, axis=1))
```

```python
import functools

import numpy as np
import jax
import jax.numpy as jnp
from jax import lax
from jax.experimental import pallas as pl
from jax.experimental.pallas import tpu as pltpu

F32 = jnp.float32
BF16 = jnp.bfloat16

D_MODEL = 1024
DEPTH = 4
GRID_W = 64
HD = 64
N_QH = 8
N_KV = 2
QW = N_QH * HD
KVW = N_KV * HD
B_HEADS = 4
B_DK = 64
B_DV = 128
B_RANK = 16
GLA_NORMALIZER = 16.0
D_HEADS = 4
D_DK = 128
D_DV = 128
CONV_K = 5
WINDOW = 128
Q_BLOCK = 128
CHUNK = 64
ROPE_THETA = 10000.0
ROT_AXIS = HD // 2
EPS = 1e-6
LANE = 128
HALO = 8

ROW_TILE = 256
OUT_TILE = 1024
REC_TILE = 256
VMEM_LIMIT = 48 * 1024 * 1024

E_Q, E_G, E_BV, E_BG, E_BQ, E_BK, E_K, E_V, E_R, E_W = 0, 512, 1024, 1536, 2048, 2304, 2560, 2688, 2816, 2944
O_DQKV, O_Q, O_G, O_DG, O_K, O_V, O_AB, O_W = 0, 1536, 2048, 2560, 3072, 3200, 3328, 3456
assert E_V == E_K + KVW and O_V == O_K + KVW


def _sigmoid(x):
    return 1.0 / (1.0 + jnp.exp(-x))


def _silu(x):
    return x * _sigmoid(x)


def _softplus(x):
    return jnp.maximum(x, 0.0) + jnp.log(1.0 + jnp.exp(-jnp.abs(x)))


def _dot(a, b):
    return jnp.dot(a.astype(BF16), b.astype(BF16), preferred_element_type=F32)


def _dot_nt(a, b):
    return lax.dot_general(a.astype(BF16), b.astype(BF16), (((1,), (1,)), ((), ())),
                           preferred_element_type=F32)


def _dot_tn(a, b):
    return lax.dot_general(a.astype(BF16), b.astype(BF16), (((0,), (0,)), ((), ())),
                           preferred_element_type=F32)


def _split(x):
    hi = x.astype(BF16)
    lo = (x - hi.astype(F32)).astype(BF16)
    return hi, lo


def _dot_exact_lhs(m, x):
    hi, lo = _split(x)
    return _dot(m, hi) + _dot(m, lo)


def _dot_tn_exact_rhs(x, m):
    hi, lo = _split(x)
    return _dot_tn(hi, m) + _dot_tn(lo, m)


def _iota(shape, dim):
    return lax.broadcasted_iota(jnp.int32, shape, dim)


def _cparams(sem):
    return pltpu.CompilerParams(dimension_semantics=sem, vmem_limit_bytes=VMEM_LIMIT)


def _mod_kernel(c_ref, w_ref, b_ref, o_ref):
    cs = _silu(c_ref[...])
    o_ref[0] = _dot(cs, w_ref[0]) + b_ref[0]


N_COND = 8


def _modulation(cond, w_mod, b_mod):
    return pl.pallas_call(
        _mod_kernel,
        grid=(DEPTH, 3),
        in_specs=[
            pl.BlockSpec((N_COND, D_MODEL), lambda l, j: (0, 0)),
            pl.BlockSpec((1, D_MODEL, D_MODEL), lambda l, j: (l, 0, j)),
            pl.BlockSpec((1, 1, D_MODEL), lambda l, j: (l, 0, j)),
        ],
        out_specs=pl.BlockSpec((1, N_COND, D_MODEL), lambda l, j: (l * 3 + j, 0, 0)),
        out_shape=jax.ShapeDtypeStruct((DEPTH * 3, N_COND, D_MODEL), F32),
        compiler_params=_cparams(("parallel", "parallel")),
        name="modulation",
    )(cond, w_mod, b_mod.reshape(DEPTH, 1, 3 * D_MODEL)).reshape(DEPTH * 3 * N_COND, 1, D_MODEL)


def _mod_row(layer, part, row):
    return (layer * 3 + part) * N_COND + row


def _mod_spec(layer, part, first_row, tiles_per_batch):
    base = _mod_row(layer, part, first_row)
    if tiles_per_batch is None:
        return pl.BlockSpec((1, 1, D_MODEL), lambda i: (base, 0, 0))
    return pl.BlockSpec((1, 1, D_MODEL), lambda i: (base + i // tiles_per_batch, 0, 0))


def _head_rms(x, bd, w):
    ss = _dot_exact_lhs_right(x * x, bd)
    return x * lax.rsqrt(ss * (1.0 / HD) + EPS) * w


def _dot_exact_lhs_right(x, m):
    hi, lo = _split(x)
    return _dot(hi, m) + _dot(lo, m)


def _rope(x, cos, sa, sb):
    return x * cos + pltpu.roll(x, LANE - ROT_AXIS // 2, axis=1) * sa + pltpu.roll(x, ROT_AXIS // 2, axis=1) * sb


def _short_conv_norm(zc, zp, zn, first, last, w_ref, xe_ref, o_ref):
    rows = zc.shape[0]
    xe_ref[0:HALO, :] = jnp.where(first, 0.0, zp)
    xe_ref[HALO:HALO + rows, :] = zc
    xe_ref[HALO + rows:, :] = jnp.where(last, 0.0, zn)
    qk_w = 2 * D_HEADS * D_DK

    def one_head(h):
        cols = slice(h * D_DK, (h + 1) * D_DK)
        acc = None
        for j in range(CONV_K):
            start = HALO - CONV_K // 2 + j
            term = xe_ref[start:start + rows, cols] * w_ref[j:j + 1, cols]
            acc = term if acc is None else acc + term
        y = _silu(acc)
        if h * D_DK < qk_w:
            y = y * lax.rsqrt(jnp.sum(y * y, axis=-1, keepdims=True) + EPS)
            if h < D_HEADS:
                y = y * (D_DK ** -0.5)
        o_ref[:, cols] = y

    return [functools.partial(one_head, h) for h in range(3 * D_HEADS)]


def _inproj_kernel(*refs, even, rope, tiles_per_batch):
    x_ref, shift_ref, scale_ref, g_ref, w_ref = refs[:5]
    pos = 5
    if even:
        qn_ref, kn_ref, bd_ref = refs[pos:pos + 3]
        pos += 3
    else:
        xp_ref, xn_ref, cw_ref = refs[pos:pos + 3]
        pos += 3
    if rope:
        cos_ref, sa_ref, sb_ref = refs[pos:pos + 3]
        pos += 3
    o_ref = refs[pos]
    if not even:
        xe_ref = refs[pos + 1]

    def modulated(x):
        y = x * lax.rsqrt(jnp.mean(x * x, axis=-1, keepdims=True) + EPS) * g_ref[...]
        return (y * (1.0 + scale_ref[0]) + shift_ref[0]).astype(BF16)

    h = modulated(x_ref[...])

    q_off, k_off = (E_Q, E_K) if even else (O_Q, O_K)
    width = E_W if even else O_W
    conv_w = 0 if even else 3 * D_HEADS * D_DK
    if not even:
        halo = modulated(jnp.concatenate([xp_ref[...], xn_ref[...]], axis=0))
        zc = jnp.dot(h, w_ref[:, 0:conv_w], preferred_element_type=F32)
        zh = jnp.dot(halo, w_ref[:, 0:conv_w], preferred_element_type=F32)

    def transform(z, is_q):
        if even:
            z = _head_rms(z, bd_ref[...], qn_ref[...] if is_q else kn_ref[...])
        if rope:
            z = _rope(z, cos_ref[...], sa_ref[...], sb_ref[...])
        return z

    wide = 2 * LANE
    pending = []
    spans = [(conv_w, width)]
    if even or rope:
        starts = [q_off + i * wide for i in range(QW // wide)] + [k_off]
        pending = [(off, jnp.dot(h, w_ref[:, off:off + wide], preferred_element_type=F32)) for off in starts]
        spans, cur = [], conv_w
        for off in sorted(starts):
            if off > cur:
                spans.append((cur, off))
            cur = off + wide
        if cur < width:
            spans.append((cur, width))
    def project(s, e):
        o_ref[:, s:e] = jnp.dot(h, w_ref[:, s:e], preferred_element_type=F32)

    step = wide
    dots = [functools.partial(project, s, min(s + step, b)) for a, b in spans for s in range(a, b, step)]
    vpu = []
    if not even:
        i = pl.program_id(0)
        first = (i % tiles_per_batch) == 0
        last = (i % tiles_per_batch) == tiles_per_batch - 1
        vpu = _short_conv_norm(zc, zh[:HALO], zh[HALO:], first, last, cw_ref, xe_ref, o_ref)
    for k in range(max(len(dots), len(vpu))):
        if k < len(vpu):
            vpu[k]()
        if k < len(dots):
            dots[k]()
    for off, z in pending:
        if off == k_off:
            o_ref[:, off:off + LANE] = transform(z[:, :LANE], False)
            o_ref[:, off + LANE:off + wide] = z[:, LANE:]
        else:
            for i in range(wide // LANE):
                o_ref[:, off + i * LANE:off + (i + 1) * LANE] = transform(z[:, i * LANE:(i + 1) * LANE], True)


def _inproj(x, mod, layer, cond_row, g_pre, w, *, even, rope_tabs=None, qn=None, kn=None, bd=None, conv_w=None,
            rows_per_batch):
    T = x.shape[0]
    width = w.shape[1]
    tpb = rows_per_batch // ROW_TILE
    rope = rope_tabs is not None
    mod_tpb = None if cond_row is None else tpb
    ins = [x, mod, mod, g_pre.reshape(1, D_MODEL), w]
    specs = [
        pl.BlockSpec((ROW_TILE, D_MODEL), lambda i: (i, 0)),
        _mod_spec(layer, 0, cond_row or 0, mod_tpb),
        _mod_spec(layer, 1, cond_row or 0, mod_tpb),
        pl.BlockSpec((1, D_MODEL), lambda i: (0, 0)),
        pl.BlockSpec((D_MODEL, width), lambda i: (0, 0)),
    ]
    if even:
        ins += [qn, kn, bd]
        specs += [pl.BlockSpec((1, LANE), lambda i: (0, 0)),
                  pl.BlockSpec((1, LANE), lambda i: (0, 0)),
                  pl.BlockSpec((LANE, LANE), lambda i: (0, 0))]
        scratch = []
    else:
        hb = ROW_TILE // HALO
        n_hb = T // HALO
        cw = conv_w.shape[1]
        ins += [x, x, conv_w]
        specs += [pl.BlockSpec((HALO, D_MODEL), lambda i: (jnp.maximum(i * hb - 1, 0), 0)),
                  pl.BlockSpec((HALO, D_MODEL), lambda i: (jnp.minimum((i + 1) * hb, n_hb - 1), 0)),
                  pl.BlockSpec((CONV_K, cw), lambda i: (0, 0))]
        scratch = [pltpu.VMEM((ROW_TILE + 2 * HALO, cw), F32)]
    if rope:
        ins += list(rope_tabs)
        specs += [pl.BlockSpec((ROW_TILE, LANE), lambda i: (i % tpb, 0))] * 3
    return pl.pallas_call(
        functools.partial(_inproj_kernel, even=even, rope=rope, tiles_per_batch=tpb),
        grid=(T // ROW_TILE,),
        in_specs=specs,
        out_specs=pl.BlockSpec((ROW_TILE, width), lambda i: (i, 0)),
        out_shape=jax.ShapeDtypeStruct((T, width), F32),
        scratch_shapes=scratch,
        compiler_params=_cparams(("parallel",)),
        name="inproj_even" if even else "inproj_odd",
    )(*ins)


LOG2E = 1.4426950408889634
SCORE_SCALE = HD ** -0.5 * LOG2E


def _values_with_ones(v):
    lower = _iota(v.shape, 1) < HD
    return jnp.where(lower, v, 1.0).astype(BF16), jnp.where(lower, 1.0, v).astype(BF16)


def _pipelined_heads(scores, finish, lookahead, lower, g_ref, o_ref):
    heads = [(c, e) for c in range(QW // LANE) for e in range(N_KV)]
    queue = [scores(*h) for h in heads[:lookahead]]
    halves = {}
    for i, (c, e) in enumerate(heads):
        cur = queue.pop(0)
        if i + lookahead < len(heads):
            queue.append(scores(*heads[i + lookahead]))
        halves[e] = finish(c, e, *cur)
        if e == N_KV - 1:
            oc = jnp.where(lower, halves[0], halves[1])
            gated = oc * _silu(g_ref[:, c * LANE:(c + 1) * LANE])
            o_ref[:, c * LANE:(c + 1) * LANE] = gated.astype(o_ref.dtype)


def _attn_dense_kernel(*refs, has_ctx, has_sink, tq, lookahead):
    q_ref, k_ref, v_ref, g_ref = refs[:4]
    pos = 4
    if has_ctx:
        kc_ref, vc_ref = refs[pos:pos + 2]
        pos += 2
    if has_sink:
        sink_ref = refs[pos]
        pos += 1
    o_ref, kb_ref, vb0_ref, vb1_ref = refs[pos:pos + 4]

    @pl.when(pl.program_id(1) == 0)
    def _():
        kb_ref[...] = k_ref[...].astype(BF16)
        vb0_ref[...], vb1_ref[...] = _values_with_ones(v_ref[...])

    kb = kb_ref[...]
    vb = (vb0_ref[...], vb1_ref[...])
    if has_ctx:
        kcb = kc_ref[0].astype(BF16)
        vcb = _values_with_ones(vc_ref[0])
    lower = _iota((tq, LANE), 1) < HD

    def scores(c, e):
        qc = q_ref[:, c * LANE:(c + 1) * LANE] * SCORE_SCALE
        qe = jnp.where(lower if e == 0 else jnp.logical_not(lower), qc, 0.0).astype(BF16)
        return _dot_nt(qe, kb), (_dot_nt(qe, kcb) if has_ctx else None)

    def finish(c, e, s, sc):
        m = jnp.max(s, axis=-1, keepdims=True)
        if has_ctx:
            m = jnp.maximum(m, jnp.max(sc, axis=-1, keepdims=True))
        if has_sink:
            sk = sink_ref[c * N_KV + e] * LOG2E
            m = jnp.maximum(m, sk)
        o = _dot(jnp.exp2((s - m).astype(BF16)), vb[e])
        if has_ctx:
            o = o + _dot(jnp.exp2(sc - m), vcb[e])
        l = pltpu.roll(o, HD, axis=1)
        if has_sink:
            l = l + jnp.exp2(sk - m)
        return o / l

    _pipelined_heads(scores, finish, lookahead, lower, g_ref, o_ref)


def _attn_dense(z, *, n_batch, seq, tq, q_off, g_off, k_off, v_off, ctx=None, sink=None):
    T = z.shape[0]
    nq = seq // tq
    has_ctx = ctx is not None
    has_sink = sink is not None
    ins = [z, z, z, z]
    specs = [
        pl.BlockSpec((tq, QW), lambda b, i: (b * nq + i, q_off // QW)),
        pl.BlockSpec((seq, KVW), lambda b, i: (b, k_off // KVW)),
        pl.BlockSpec((seq, KVW), lambda b, i: (b, v_off // KVW)),
        pl.BlockSpec((tq, QW), lambda b, i: (b * nq + i, g_off // QW)),
    ]
    if has_ctx:
        ins += list(ctx)
        n_ctx = ctx[0].shape[1]
        specs += [pl.BlockSpec((1, n_ctx, KVW), lambda b, i: (b, 0, 0))] * 2
    if has_sink:
        ins.append(sink)
        specs.append(pl.BlockSpec(memory_space=pltpu.SMEM))
    return pl.pallas_call(
        functools.partial(_attn_dense_kernel, has_ctx=has_ctx, has_sink=has_sink, tq=tq,
                          lookahead=N_QH if seq <= 1024 else 2),
        grid=(n_batch, nq),
        in_specs=specs,
        out_specs=pl.BlockSpec((tq, QW), lambda b, i: (b * nq + i, 0)),
        out_shape=jax.ShapeDtypeStruct((T, QW), BF16),
        scratch_shapes=[pltpu.VMEM((seq, KVW), BF16)] * 3,
        compiler_params=_cparams(("arbitrary", "arbitrary")),
        name="attn_dense",
    )(*ins)


def _attn_band_kernel(q_ref, kp_ref, kc_ref, kn_ref, vp_ref, vc_ref, vn_ref, g_ref, kx_ref, vx_ref, sink_ref,
                      o_ref, *, seq):
    n = pl.program_id(1)
    tq = Q_BLOCK
    kcat = jnp.concatenate([kp_ref[...], kc_ref[...], kn_ref[...]], axis=0).astype(BF16)
    vcat = _values_with_ones(jnp.concatenate([vp_ref[...], vc_ref[...], vn_ref[...]], axis=0))
    kxb = kx_ref[0].astype(BF16)
    vxb = _values_with_ones(vx_ref[0])
    qpos = n * tq + _iota((tq, 3 * tq), 0)
    kpos = (n - 1) * tq + _iota((tq, 3 * tq), 1)
    valid = (jnp.abs(qpos - kpos) <= WINDOW) & (kpos >= 0) & (kpos < seq)
    lower = _iota((tq, LANE), 1) < HD

    def scores(c, e):
        qc = q_ref[:, c * LANE:(c + 1) * LANE] * SCORE_SCALE
        qe = jnp.where(lower if e == 0 else jnp.logical_not(lower), qc, 0.0).astype(BF16)
        return _dot_nt(qe, kcat), _dot_nt(qe, kxb)

    def finish(c, e, s, sx):
        s = jnp.where(valid, s, -jnp.inf)
        sk = sink_ref[c * N_KV + e] * LOG2E
        m = jnp.maximum(jnp.maximum(jnp.max(s, axis=-1, keepdims=True),
                                    jnp.max(sx, axis=-1, keepdims=True)), sk)
        o = _dot(jnp.exp2(s - m), vcat[e]) + _dot(jnp.exp2(sx - m), vxb[e])
        l = pltpu.roll(o, HD, axis=1) + jnp.exp2(sk - m)
        return o / l

    _pipelined_heads(scores, finish, N_QH, lower, g_ref, o_ref)


def _attn_band(z, ctx, sink, *, n_batch, seq):
    T = z.shape[0]
    nq = seq // Q_BLOCK
    n_ctx = ctx[0].shape[1]

    def kv_spec(off, delta):
        return pl.BlockSpec(
            (Q_BLOCK, KVW),
            lambda b, i: (b * nq + jnp.clip(i + delta, 0, nq - 1), off // KVW))

    specs = [
        pl.BlockSpec((Q_BLOCK, QW), lambda b, i: (b * nq + i, O_Q // QW)),
        kv_spec(O_K, -1), kv_spec(O_K, 0), kv_spec(O_K, 1),
        kv_spec(O_V, -1), kv_spec(O_V, 0), kv_spec(O_V, 1),
        pl.BlockSpec((Q_BLOCK, QW), lambda b, i: (b * nq + i, O_G // QW)),
        pl.BlockSpec((1, n_ctx, KVW), lambda b, i: (b, 0, 0)),
        pl.BlockSpec((1, n_ctx, KVW), lambda b, i: (b, 0, 0)),
        pl.BlockSpec(memory_space=pltpu.SMEM),
    ]
    return pl.pallas_call(
        functools.partial(_attn_band_kernel, seq=seq),
        grid=(n_batch, nq),
        in_specs=specs,
        out_specs=pl.BlockSpec((Q_BLOCK, QW), lambda b, i: (b * nq + i, 0)),
        out_shape=jax.ShapeDtypeStruct((T, QW), BF16),
        compiler_params=_cparams(("parallel", "parallel")),
        name="attn_band",
    )(z, z, z, z, z, z, z, z, ctx[0], ctx[1], sink)


def _gla_tile(dirs, n_chunk):
    rows = n_chunk * CHUNK
    shift = CHUNK.bit_length() - 1
    ii = _iota((rows, rows), 0)
    jj = _iota((rows, rows), 1)
    same = lax.shift_right_logical(ii, shift) == lax.shift_right_logical(jj, shift)
    sel = jnp.where(lax.shift_right_logical(_iota((rows, n_chunk * B_DV), 0), shift)
                    == lax.shift_right_logical(_iota((rows, n_chunk * B_DV), 1), B_DV.bit_length() - 1),
                    1.0, 0.0).astype(BF16)
    ci = _iota((CHUNK, CHUNK), 0)
    cj = _iota((CHUNK, CHUNK), 1)
    lower = _iota((CHUNK, LANE), 1) < B_DK
    n_pair = B_HEADS * B_DK // LANE
    per_pair = LANE // B_DK

    gs = []
    for (_, _, _, r_ref, up, bias, _, _, _) in dirs:
        gk = _dot(r_ref[...], up) + bias
        gs.append(-_softplus(-gk) * (1.0 / GLA_NORMALIZER))
    bs = []
    for g, d in zip(gs, dirs):
        tri = jnp.where(same & ((jj >= ii) if d[8] else (jj <= ii)), 1.0, 0.0).astype(BF16)
        bs.append(_dot_exact_lhs(tri, g))
    decargs = [_dot_tn_exact_rhs(g, sel) for g in gs]

    units = []
    for di, (q_ref, k_ref, v_ref, _, _, _, st_ref, o_ref, rev) in enumerate(dirs):
        b = bs[di]
        incl = (cj >= ci) if rev else (cj <= ci)
        order = range(n_chunk - 1, -1, -1) if rev else range(n_chunk)
        for c in order:
            r0 = c * CHUNK
            bc = b[r0:r0 + CHUNK]
            btot = bc[0:1, :] if rev else bc[CHUNK - 1:CHUNK, :]
            qt = q_ref[r0:r0 + CHUNK, :] * (B_DK ** -0.5) * jnp.exp(bc)
            kc = k_ref[r0:r0 + CHUNK, :]
            kt = kc * jnp.exp(-bc)
            kend = kc * jnp.exp(btot - bc)
            for p in range(n_pair):
                for e in range(per_pair):
                    h = p * per_pair + e
                    qm = jnp.where(lower if e == 0 else jnp.logical_not(lower), qt[:, p * LANE:(p + 1) * LANE], 0.0)
                    units.append(dict(di=di, c=c, p=p, e=e, incl=incl, qm=qm.astype(BF16),
                                      kt=kt[:, p * LANE:(p + 1) * LANE].astype(BF16),
                                      kend=kend[:, p * LANE:(p + 1) * LANE].astype(BF16),
                                      v=v_ref[r0:r0 + CHUNK, h * B_DV:(h + 1) * B_DV].astype(BF16)))
    for u in units:
        u["att"] = jnp.where(u["incl"], _dot_nt(u["qm"], u["kt"]), 0.0).astype(BF16)
    for u in units:
        u["upd"] = _dot_tn(u["kend"], u["v"])[u["e"] * B_DK:(u["e"] + 1) * B_DK]
    for di, d in enumerate(dirs):
        st = d[6][...]
        mine = [u for u in units if u["di"] == di]
        for k0 in range(0, len(mine), B_HEADS):
            grp = mine[k0:k0 + B_HEADS]
            c = grp[0]["c"]
            for u in grp:
                u["st"] = st[u["p"] * LANE:(u["p"] + 1) * LANE].astype(BF16)
            dec = jnp.exp(decargs[di][:, c * B_DV:(c + 1) * B_DV])
            st = dec * st + jnp.concatenate([u["upd"] for u in grp], axis=0)
        d[6][...] = st
    for u in units:
        u["qs"] = _dot(u["qm"], u["st"])
    for u in units:
        u["o"] = _dot(u["att"], u["v"]) + u["qs"]
    for di, d in enumerate(dirs):
        mine = [u for u in units if u["di"] == di]
        for k0 in range(0, len(mine), B_HEADS):
            grp = mine[k0:k0 + B_HEADS]
            r0 = grp[0]["c"] * CHUNK
            d[7][r0:r0 + CHUNK, :] = jnp.concatenate([u["o"] for u in grp], axis=1)


def _gla_kernel(*refs, has_s0, n_chunk):
    qf, kf, vf, rf, qb, kb, vb, rb, up_ref, bias_ref = refs[:10]
    pos = 10
    if has_s0:
        s0f, s0b = refs[pos:pos + 2]
        pos += 2
    of_ref, ob_ref, sf_ref, sb_ref, stf, stb = refs[pos:pos + 6]
    n = pl.program_id(1)

    @pl.when(n == 0)
    def _():
        if has_s0:
            stf[...] = s0f[0]
            stb[...] = s0b[0]
        else:
            stf[...] = jnp.zeros_like(stf)
            stb[...] = jnp.zeros_like(stb)

    _gla_tile([(qf, kf, vf, rf, up_ref[0], bias_ref[0], stf, of_ref, False),
               (qb, kb, vb, rb, up_ref[1], bias_ref[1], stb, ob_ref, True)], n_chunk)

    @pl.when(n == pl.num_programs(1) - 1)
    def _():
        sf_ref[0] = stf[...]
        sb_ref[0] = stb[...]


def _gla(z, up_ext, bias, s0, *, n_batch, seq):
    T = z.shape[0]
    nt = seq // REC_TILE
    has_s0 = s0 is not None
    sw = B_HEADS * B_DK

    def fwd(b, i):
        return b * nt + i

    def bwd(b, i):
        return b * nt + nt - 1 - i

    def specs_for(row):
        return [
            pl.BlockSpec((REC_TILE, sw), lambda b, i: (row(b, i), E_BQ // sw)),
            pl.BlockSpec((REC_TILE, sw), lambda b, i: (row(b, i), E_BK // sw)),
            pl.BlockSpec((REC_TILE, QW), lambda b, i: (row(b, i), E_BV // QW)),
            pl.BlockSpec((REC_TILE, LANE), lambda b, i: (row(b, i), E_R // LANE)),
        ]

    ins = [z] * 8 + [up_ext, bias]
    specs = specs_for(fwd) + specs_for(bwd) + [
        pl.BlockSpec((2, LANE, sw), lambda b, i: (0, 0, 0)),
        pl.BlockSpec((2, 1, sw), lambda b, i: (0, 0, 0)),
    ]
    if has_s0:
        ins += [s0[0], s0[1]]
        specs += [pl.BlockSpec((1, sw, B_DV), lambda b, i: (b, 0, 0))] * 2
    st_shape = jax.ShapeDtypeStruct((n_batch, sw, B_DV), F32)
    o_shape = jax.ShapeDtypeStruct((T, QW), F32)
    return pl.pallas_call(
        functools.partial(_gla_kernel, has_s0=has_s0, n_chunk=REC_TILE // CHUNK),
        grid=(n_batch, nt),
        in_specs=specs,
        out_specs=[
            pl.BlockSpec((REC_TILE, QW), lambda b, i: (fwd(b, i), 0)),
            pl.BlockSpec((REC_TILE, QW), lambda b, i: (bwd(b, i), 0)),
            pl.BlockSpec((1, sw, B_DV), lambda b, i: (b, 0, 0)),
            pl.BlockSpec((1, sw, B_DV), lambda b, i: (b, 0, 0)),
        ],
        out_shape=[o_shape, o_shape, st_shape, st_shape],
        scratch_shapes=[pltpu.VMEM((sw, B_DV), F32), pltpu.VMEM((sw, B_DV), F32)],
        compiler_params=_cparams(("arbitrary", "arbitrary")),
        name="gla",
    )(*ins)


def _dot3_stacked(lhs_list, rhs, expand=None):
    rh, rl = _split(rhs)
    if expand is not None:
        rh, rl = expand(rh), expand(rl)
    parts = [_split(x) for x in lhs_list]
    his = [p[0] for p in parts]
    los = [p[1] for p in parts]
    n = lhs_list[0].shape[0]
    k = len(lhs_list)
    r1 = _dot(jnp.concatenate(his + los, axis=0), rh)
    r2 = _dot(jnp.concatenate(his, axis=0) if k > 1 else his[0], rl)
    return [r1[i * n:(i + 1) * n] + (r1[(k + i) * n:(k + i + 1) * n] + r2[i * n:(i + 1) * n]) for i in range(k)]


def _delta_tile(dirs, alog, dtb, n_chunk):
    C = CHUNK
    W = D_HEADS * C
    hk = D_HEADS * D_DK
    sh = C.bit_length() - 1
    ii = _iota((C, C), 0)
    jj = _iota((C, C), 1)
    pi = _iota((C, W), 0)
    pl_ = _iota((C, W), 1)
    pj = pl_ & (C - 1)
    pblk = lax.shift_right_logical(pl_, sh)
    eye_p = jnp.where(pi == pj, 1.0, 0.0)
    bdm = (lax.shift_right_logical(_iota((W, W), 0), sh) == lax.shift_right_logical(_iota((W, W), 1), sh))
    bdk = (lax.shift_right_logical(_iota((W, hk), 0), sh)
           == lax.shift_right_logical(_iota((W, hk), 1), D_DK.bit_length() - 1))
    ones_cc = jnp.ones((C, C), BF16)

    def pack(cols):
        out = jnp.broadcast_to(cols[D_HEADS - 1], (C, W))
        for h in range(D_HEADS - 2, -1, -1):
            out = jnp.where(pblk == h, jnp.broadcast_to(cols[h], (C, W)), out)
        return out

    def wide(cols):
        return jnp.concatenate([jnp.broadcast_to(c, (C, D_DK)) for c in cols], axis=1)

    def block_diag(x):
        return jnp.where(bdm, jnp.concatenate([x] * D_HEADS, axis=0), jnp.zeros((), x.dtype))

    units = []
    for d, (x_ref, ab_ref, st_ref, o_ref, rev) in enumerate(dirs):
        incl = (jj >= ii) if rev else (jj <= ii)
        tri = jnp.where(incl, 1.0, 0.0).astype(BF16)
        incl_p = (pj >= pi) if rev else (pj <= pi)
        strict_p = (pj > pi) if rev else (pj < pi)
        before_p = (pj <= pi) if rev else (pj >= pi)
        order = range(n_chunk - 1, -1, -1) if rev else range(n_chunk)
        for c in order:
            r0 = c * C
            ab = ab_ref[r0:r0 + C, :]
            gall = -jnp.exp(alog) * _softplus(ab + dtb)
            beta_all = _sigmoid(ab)
            gc_all = _dot_exact_lhs(tri, gall)
            ig = [d * D_HEADS + h for h in range(D_HEADS)]
            gcols = [gc_all[:, i:i + 1] for i in ig]
            g_p = pack([gall[:, i:i + 1] for i in ig])
            gc_p = pack(gcols)
            gr_p = _dot_exact_lhs(ones_cc, jnp.where(before_p, g_p, 0.0))
            decay = jnp.exp(jnp.where(incl_p, gc_p - gr_p, -jnp.inf))
            gc_w = wide(gcols)
            beta_w = wide([beta_all[:, 2 * D_HEADS + i:2 * D_HEADS + i + 1] for i in ig])
            glast_w = gc_w[0:1, :] if rev else gc_w[C - 1:C, :]
            egc_w = jnp.exp(gc_w)
            q_all = x_ref[r0:r0 + C, 0:hk]
            k_all = x_ref[r0:r0 + C, hk:2 * hk]
            v_all = x_ref[r0:r0 + C, 2 * hk:3 * hk]
            kb_all = k_all * beta_w
            k_bd = jnp.where(bdk, jnp.concatenate([k_all] * D_HEADS, axis=0), 0.0)
            r = _dot_nt(jnp.concatenate([kb_all, q_all], axis=0), k_bd)
            p = -jnp.where(strict_p, r[:C] * decay, 0.0)
            vb_all = v_all * beta_w
            kbe_all = kb_all * egc_w
            rhs = jnp.concatenate(
                [jnp.concatenate([vb_all[:, h * D_DV:(h + 1) * D_DV], kbe_all[:, h * D_DK:(h + 1) * D_DK]], axis=1)
                 for h in range(D_HEADS)], axis=0)
            units.append(dict(d=d, r0=r0, st_ref=st_ref, o_ref=o_ref, p=p, acc=eye_p + p, att=r[C:] * decay,
                              rhs=rhs, qdec=q_all * egc_w, kdec=k_all * jnp.exp(glast_w - gc_w),
                              eglast=jnp.exp(glast_w)))
    per_pair = LANE // C
    half = lax.shift_right_logical(_iota((C, LANE), 1), sh)

    def square_first(us):
        for u in us:
            u["p"] = _dot3_stacked([u["p"]], u["p"], block_diag)[0]

    def square_and_extend(us):
        for u in us:
            pp, ap = _dot3_stacked([u["p"], u["acc"]], u["p"], block_diag)
            u["acc"] = u["acc"] + ap
            u["p"] = pp

    def extend_last(us):
        for u in us:
            u["acc"] = u["acc"] + _dot3_stacked([u["acc"]], u["p"], block_diag)[0]

    def solve(us):
        for u in us:
            u["sol"] = []
            for c in range(D_HEADS // per_pair):
                lanes = slice(c * LANE, (c + 1) * LANE)
                lhs = [jnp.where(half == e, u["acc"][:, lanes], 0.0) for e in range(per_pair)]
                u["sol"] += _dot3_stacked(lhs, u["rhs"][c * LANE:(c + 1) * LANE])

    n_mid = 0
    m = 2
    while 2 * m < C:
        n_mid += 1
        m *= 2
    local_stages = [square_first] + [square_and_extend] * n_mid + [extend_last, solve]

    def read_state(cur):
        for u in cur:
            u["s"] = [u["st_ref"][h] for h in range(D_HEADS)]
            u["vnew"], u["qs"] = [], []
            for h in range(D_HEADS):
                lhs = jnp.concatenate([u["sol"][h][:, D_DV:], u["qdec"][:, h * D_DK:(h + 1) * D_DK]], axis=0)
                r = _dot(lhs, u["s"][h])
                u["vnew"].append(u["sol"][h][:, :D_DV] - r[:C])
                u["qs"].append(r[C:])

    def write_state(cur):
        for u in cur:
            o_pairs = []
            for c in range(D_HEADS // per_pair):
                lanes = slice(c * LANE, (c + 1) * LANE)
                att_rows = jnp.concatenate(
                    [jnp.where(half == e, u["att"][:, lanes], 0.0) for e in range(per_pair)],
                    axis=0)
                o_pairs.append(_dot(att_rows, jnp.concatenate(u["vnew"][c * per_pair:(c + 1) * per_pair], axis=0)))
            o_all = jnp.concatenate(o_pairs, axis=0)
            outs = []
            for h in range(D_HEADS):
                outs.append(u["qs"][h] + o_all[h * C:(h + 1) * C])
                u["st_ref"][h] = (u["s"][h] * u["eglast"][:, h * D_DK:(h + 1) * D_DK]
                                  + _dot_tn(u["kdec"][:, h * D_DK:(h + 1) * D_DK], u["vnew"][h]))
            u["o_ref"][u["r0"]:u["r0"] + C, :] = jnp.concatenate(outs, axis=1)

    per_dir = [[u for u in units if u["d"] == d] for d in range(len(dirs))]
    steps = [[lst[i] for lst in per_dir] for i in range(n_chunk)]
    n_first = n_chunk // 2
    for stage in local_stages:
        stage([u for cur in steps[:n_first] for u in cur])
    pending = [functools.partial(f, cur) for cur in steps[:n_first] for f in (read_state, write_state)]
    for stage in local_stages:
        stage([u for cur in steps[n_first:] for u in cur])
        if pending:
            pending.pop(0)()
    for f in pending:
        f()
    for cur in steps[n_first:]:
        read_state(cur)
        write_state(cur)


def _delta_kernel(*refs, has_s0, n_chunk):
    xf, af, xb, ab_, alog_ref, dtb_ref = refs[:6]
    pos = 6
    if has_s0:
        s0f, s0b = refs[pos:pos + 2]
        pos += 2
    of_ref, ob_ref, sf_ref, sb_ref, stf, stb = refs[pos:pos + 6]
    n = pl.program_id(1)

    @pl.when(n == 0)
    def _():
        if has_s0:
            stf[...] = s0f[0]
            stb[...] = s0b[0]
        else:
            stf[...] = jnp.zeros_like(stf)
            stb[...] = jnp.zeros_like(stb)

    _delta_tile([(xf, af, stf, of_ref, False), (xb, ab_, stb, ob_ref, True)],
                alog_ref[...], dtb_ref[...], n_chunk)

    @pl.when(n == pl.num_programs(1) - 1)
    def _():
        sf_ref[0] = stf[...]
        sb_ref[0] = stb[...]


def _delta(z, alog_row, dtb_row, s0, *, n_batch, seq):
    T = z.shape[0]
    nt = seq // REC_TILE
    has_s0 = s0 is not None
    cw = 3 * D_HEADS * D_DK
    assert O_DQKV == 0 and ROW_TILE == REC_TILE

    def fwd(b, i):
        return b * nt + i

    def bwd(b, i):
        return b * nt + nt - 1 - i

    ins = [z, z, z, z, alog_row, dtb_row]
    specs = [
        pl.BlockSpec((REC_TILE, cw), lambda b, i: (fwd(b, i), 0)),
        pl.BlockSpec((REC_TILE, LANE), lambda b, i: (fwd(b, i), O_AB // LANE)),
        pl.BlockSpec((REC_TILE, cw), lambda b, i: (bwd(b, i), 0)),
        pl.BlockSpec((REC_TILE, LANE), lambda b, i: (bwd(b, i), O_AB // LANE)),
        pl.BlockSpec((1, LANE), lambda b, i: (0, 0)),
        pl.BlockSpec((1, LANE), lambda b, i: (0, 0)),
    ]
    st_block = (1, D_HEADS, D_DK, D_DV)
    if has_s0:
        ins += [s0[0], s0[1]]
        specs += [pl.BlockSpec(st_block, lambda b, i: (b, 0, 0, 0))] * 2
    st_shape = jax.ShapeDtypeStruct((n_batch, D_HEADS, D_DK, D_DV), F32)
    o_shape = jax.ShapeDtypeStruct((T, QW), F32)
    return pl.pallas_call(
        functools.partial(_delta_kernel, has_s0=has_s0, n_chunk=REC_TILE // CHUNK),
        grid=(n_batch, nt),
        in_specs=specs,
        out_specs=[
            pl.BlockSpec((REC_TILE, QW), lambda b, i: (fwd(b, i), 0)),
            pl.BlockSpec((REC_TILE, QW), lambda b, i: (bwd(b, i), 0)),
            pl.BlockSpec(st_block, lambda b, i: (b, 0, 0, 0)),
            pl.BlockSpec(st_block, lambda b, i: (b, 0, 0, 0)),
        ],
        out_shape=[o_shape, o_shape, st_shape, st_shape],
        scratch_shapes=[pltpu.VMEM((D_HEADS, D_DK, D_DV), F32), pltpu.VMEM((D_HEADS, D_DK, D_DV), F32)],
        compiler_params=_cparams(("arbitrary", "arbitrary")),
        name="delta",
    )(*ins)


def _outproj_kernel(ya_ref, of_ref, ob_ref, zg_ref, on_ref, w_ref, x_ref, gate_ref, gp_ref, o_ref):
    ob = of_ref[...] + ob_ref[...]
    zg = zg_ref[...]
    on = on_ref[...]
    parts = []
    for h in range(QW // LANE):
        seg = ob[:, h * LANE:(h + 1) * LANE]
        nrm = seg * lax.rsqrt(jnp.mean(seg * seg, axis=-1, keepdims=True) + EPS) * on
        parts.append(nrm * _silu(zg[:, h * LANE:(h + 1) * LANE]))
    yb = jnp.concatenate(parts, axis=1)
    out = _dot(ya_ref[...], w_ref[0:QW, :]) + _dot(yb, w_ref[QW:, :])
    post = out * lax.rsqrt(jnp.mean(out * out, axis=-1, keepdims=True) + EPS) * gp_ref[...]
    o_ref[...] = x_ref[...] + gate_ref[0] * post


def _outproj(ya, o_f, o_b, z, g_off, onorm, w_out, x, mod, layer, cond_row, g_post, *, rows_per_batch):
    T = x.shape[0]
    row = lambda i: (i, 0)
    gate_spec = _mod_spec(layer, 2, cond_row or 0, None if cond_row is None else rows_per_batch // OUT_TILE)
    return pl.pallas_call(
        _outproj_kernel,
        grid=(T // OUT_TILE,),
        in_specs=[
            pl.BlockSpec((OUT_TILE, QW), row),
            pl.BlockSpec((OUT_TILE, QW), row),
            pl.BlockSpec((OUT_TILE, QW), row),
            pl.BlockSpec((OUT_TILE, QW), lambda i: (i, g_off // QW)),
            pl.BlockSpec((1, LANE), lambda i: (0, 0)),
            pl.BlockSpec((2 * QW, D_MODEL), lambda i: (0, 0)),
            pl.BlockSpec((OUT_TILE, D_MODEL), row),
            gate_spec,
            pl.BlockSpec((1, D_MODEL), lambda i: (0, 0)),
        ],
        out_specs=pl.BlockSpec((OUT_TILE, D_MODEL), row),
        out_shape=jax.ShapeDtypeStruct((T, D_MODEL), F32),
        compiler_params=_cparams(("parallel",)),
        name="outproj",
    )(ya, o_f, o_b, z, onorm.reshape(1, LANE), w_out, x, mod, g_post.reshape(1, D_MODEL))


def _rope_tables(seq):
    n_rows = seq // GRID_W
    rows = jnp.repeat(jnp.arange(n_rows, dtype=F32), GRID_W)
    cols = jnp.tile(jnp.arange(GRID_W, dtype=F32), n_rows)
    inv = jnp.power(ROPE_THETA, jnp.arange(ROT_AXIS // 2, dtype=F32) * (-2.0 / ROT_AXIS))
    ang_r = rows[:, None] * inv[None, :]
    ang_c = cols[:, None] * inv[None, :]
    zero = jnp.zeros_like(ang_r)
    cos = jnp.concatenate([jnp.cos(ang_r)] * 2 + [jnp.cos(ang_c)] * 2, axis=1)
    sa = jnp.concatenate([-jnp.sin(ang_r), zero, -jnp.sin(ang_c), zero], axis=1)
    sb = jnp.concatenate([zero, jnp.sin(ang_r), zero, jnp.sin(ang_c)], axis=1)
    return tuple(jnp.tile(t, (1, LANE // HD)) for t in (cos, sa, sb))


def _regroup_cols(w):
    g = N_QH // N_KV
    lead = w.shape[:-1]
    return jnp.swapaxes(w.reshape(lead + (N_KV, g, HD)), -3, -2).reshape(lead + (QW,))


def _regroup_rows(w):
    g = N_QH // N_KV
    return jnp.swapaxes(w.reshape(w.shape[:-2] + (N_KV, g, HD, w.shape[-1])), -4, -3).reshape(w.shape)


def _prep_even_weights(w_in, w_out, gk_up):
    n = w_in.shape[0]
    w_in = w_in.astype(BF16)
    a_q, a_k, a_v, a_g, b_q, b_k, b_v, b_r, b_g = jnp.split(
        w_in, [512, 640, 768, 1280, 1536, 1792, 2304, 2336], axis=2)
    pad = jnp.zeros((n, D_MODEL, LANE - 2 * B_RANK), BF16)
    w = jnp.concatenate([_regroup_cols(a_q), _regroup_cols(a_g), b_v, b_g, b_q, b_k, a_k, a_v, b_r, pad], axis=2)
    w_out = w_out.astype(BF16)
    wo = jnp.concatenate([_regroup_rows(w_out[:, :QW]), w_out[:, QW:]], axis=1)
    zeros = jnp.zeros((n, B_RANK, B_HEADS * B_DK), BF16)
    tail = jnp.zeros((n, LANE - 2 * B_RANK, B_HEADS * B_DK), BF16)
    gk_up = gk_up.astype(BF16)
    up = jnp.stack([jnp.concatenate([gk_up[:, 0], zeros, tail], axis=1),
                    jnp.concatenate([zeros, gk_up[:, 1], tail], axis=1)], axis=1)
    return w, wo, up


def _prep_odd_weights(w_in, w_out):
    n = w_in.shape[0]
    w_in = w_in.astype(BF16)
    c_q, c_k, c_v, c_g, d_q, d_k, d_v, d_a, d_b, d_g = jnp.split(
        w_in, [512, 640, 768, 1280, 1792, 2304, 2816, 2824, 2832], axis=2)
    pad = jnp.zeros((n, D_MODEL, LANE - 4 * D_HEADS), BF16)
    w = jnp.concatenate([d_q, d_k, d_v, _regroup_cols(c_q), _regroup_cols(c_g), d_g, c_k, c_v, d_a, d_b, pad],
                        axis=2)
    w_out = w_out.astype(BF16)
    wo = jnp.concatenate([_regroup_rows(w_out[:, :QW]), w_out[:, QW:]], axis=1)
    return w, wo


def _lane_row(x):
    flat = x.reshape(-1).astype(F32)
    return jnp.zeros((1, LANE), F32).at[0, :flat.shape[0]].set(flat)


def kernel(x_prompt, x_sample, c, cache_attn_k, cache_attn_v, state_gla, cache_swa_k, cache_swa_v, state_delta,
           c_ctx, w_mod, b_mod, g_pre, g_post, w_in_even, w_out_even, qnorm_a, knorm_a, gla_gk_up, gla_gk_bias,
           gla_onorm, w_in_odd, w_out_odd, sink_c, conv_d, a_log_d, dt_bias_d, delta_onorm):
    n_p, seq_p, _ = x_prompt.shape
    n_s, seq_s, _ = x_sample.shape
    n_ctx = cache_attn_k.shape[2]
    assert (n_p * seq_p) % OUT_TILE == 0 and seq_s % OUT_TILE == 0 and seq_p % ROW_TILE == 0
    assert seq_s % ROW_TILE == 0 and seq_p % REC_TILE == 0 and seq_s % REC_TILE == 0
    assert n_s + 1 <= N_COND

    cond = jnp.zeros((N_COND, D_MODEL), F32).at[0].set(c_ctx).at[1:1 + n_s].set(c)
    mod = _modulation(cond, w_mod, b_mod)
    rope = _rope_tables(seq_s)
    bd = jnp.asarray(np.kron(np.eye(LANE // HD), np.ones((HD, HD))), BF16)
    w_even, wo_even, up_even = _prep_even_weights(w_in_even, w_out_even, gla_gk_up)
    w_odd, wo_odd = _prep_odd_weights(w_in_odd, w_out_odd)
    sinks = jnp.swapaxes(sink_c.astype(F32).reshape(-1, N_KV, N_QH // N_KV), 1, 2).reshape(-1, N_QH)

    y_p = x_prompt.reshape(n_p * seq_p, D_MODEL)
    y_s = x_sample.reshape(n_s * seq_s, D_MODEL)
    attn_k, attn_v, gla_s, swa_k, swa_v, delta_s = [], [], [], [], [], []
    for l in range(DEPTH):
        i = l // 2
        if l % 2 == 0:
            w, wo, up = w_even[i], wo_even[i], up_even[i]
            qn = jnp.tile(qnorm_a[i], LANE // HD).reshape(1, LANE)
            kn = jnp.tile(knorm_a[i], LANE // HD).reshape(1, LANE)
            bias = gla_gk_bias[i].reshape(2, 1, B_HEADS * B_DK)
            common = dict(even=True, qn=qn, kn=kn, bd=bd)
            z_p = _inproj(y_p, mod, l, None, g_pre[l], w, rows_per_batch=seq_p, **common)
            z_s = _inproj(y_s, mod, l, 1, g_pre[l], w, rows_per_batch=seq_s, rope_tabs=rope, **common)
            offs = dict(q_off=E_Q, g_off=E_G, k_off=E_K, v_off=E_V)
            ya_p = _attn_dense(z_p, n_batch=n_p, seq=seq_p, tq=seq_p, **offs)
            ctx = (cache_attn_k[:, i].reshape(n_s, n_ctx, KVW), cache_attn_v[:, i].reshape(n_s, n_ctx, KVW))
            ya_s = _attn_dense(z_s, n_batch=n_s, seq=seq_s, tq=256, ctx=ctx, **offs)
            of_p, ob_p, sf_p, sb_p = _gla(z_p, up, bias, None, n_batch=n_p, seq=seq_p)
            s0 = state_gla[:, i].reshape(n_s, 2, B_HEADS * B_DK, B_DV)
            of_s, ob_s, _, _ = _gla(z_s, up, bias, (s0[:, 0], s0[:, 1]), n_batch=n_s, seq=seq_s)
            attn_k.append(z_p[:, E_K:E_K + KVW].reshape(n_p, seq_p, N_KV, HD))
            attn_v.append(z_p[:, E_V:E_V + KVW].reshape(n_p, seq_p, N_KV, HD))
            gla_s.append(jnp.stack([sf_p, sb_p], axis=1).reshape(n_p, 2, B_HEADS, B_DK, B_DV))
            g_off, onorm = E_BG, gla_onorm[i]
        else:
            w, wo = w_odd[i], wo_odd[i]
            z_p = _inproj(y_p, mod, l, None, g_pre[l], w, even=False, conv_w=conv_d[i], rows_per_batch=seq_p)
            z_s = _inproj(y_s, mod, l, 1, g_pre[l], w, even=False, conv_w=conv_d[i], rows_per_batch=seq_s,
                          rope_tabs=rope)
            sink = sinks[i]
            ya_p = _attn_dense(z_p, n_batch=n_p, seq=seq_p, tq=seq_p, q_off=O_Q, g_off=O_G, k_off=O_K,
                               v_off=O_V, sink=sink)
            ctx = (cache_swa_k[:, i].reshape(n_s, n_ctx, KVW), cache_swa_v[:, i].reshape(n_s, n_ctx, KVW))
            ya_s = _attn_band(z_s, ctx, sink, n_batch=n_s, seq=seq_s)
            alog_row = _lane_row(a_log_d[i])
            dtb_row = _lane_row(dt_bias_d[i])
            of_p, ob_p, sf_p, sb_p = _delta(z_p, alog_row, dtb_row, None, n_batch=n_p, seq=seq_p)
            s0 = state_delta[:, i]
            of_s, ob_s, _, _ = _delta(z_s, alog_row, dtb_row, (s0[:, 0], s0[:, 1]), n_batch=n_s, seq=seq_s)
            swa_k.append(z_p[:, O_K:O_K + KVW].reshape(n_p, seq_p, N_KV, HD))
            swa_v.append(z_p[:, O_V:O_V + KVW].reshape(n_p, seq_p, N_KV, HD))
            delta_s.append(jnp.stack([sf_p, sb_p], axis=1))
            g_off, onorm = O_DG, delta_onorm[i]
        y_p = _outproj(ya_p, of_p, ob_p, z_p, g_off, onorm, wo, y_p, mod, l, None, g_post[l], rows_per_batch=seq_p)
        y_s = _outproj(ya_s, of_s, ob_s, z_s, g_off, onorm, wo, y_s, mod, l, 1, g_post[l], rows_per_batch=seq_s)
    return (y_p.reshape(n_p, seq_p, D_MODEL), y_s.reshape(n_s, seq_s, D_MODEL),
            jnp.stack(attn_k, axis=1), jnp.stack(attn_v, axis=1), jnp.stack(gla_s, axis=1),
            jnp.stack(swa_k, axis=1), jnp.stack(swa_v, axis=1), jnp.stack(delta_s, axis=1))
```

```python
import functools

import numpy as np
import jax
import jax.numpy as jnp
from jax import lax
from jax.experimental import pallas as pl
from jax.experimental.pallas import tpu as pltpu

F32 = jnp.float32
BF16 = jnp.bfloat16

D_MODEL = 1024
DEPTH = 4
GRID_W = 64
HD = 64
N_QH = 8
N_KV = 2
QW = N_QH * HD
KVW = N_KV * HD
B_HEADS = 4
B_DK = 64
B_DV = 128
B_RANK = 16
GLA_NORMALIZER = 16.0
D_HEADS = 4
D_DK = 128
D_DV = 128
CONV_K = 5
WINDOW = 128
Q_BLOCK = 128
CHUNK = 64
ROPE_THETA = 10000.0
ROT_AXIS = HD // 2
EPS = 1e-6
LANE = 128
HALO = 8

ROW_TILE = 256
OUT_TILE = 1024
REC_TILE = 256
VMEM_LIMIT = 48 * 1024 * 1024

E_Q, E_G, E_BV, E_BG, E_BQ, E_BK, E_K, E_V, E_R, E_W = 0, 512, 1024, 1536, 2048, 2304, 2560, 2688, 2816, 2944
O_DQKV, O_Q, O_G, O_DG, O_K, O_V, O_AB, O_W = 0, 1536, 2048, 2560, 3072, 3200, 3328, 3456
assert E_V == E_K + KVW and O_V == O_K + KVW


def _sigmoid(x):
    return 1.0 / (1.0 + jnp.exp(-x))


def _silu(x):
    return x * _sigmoid(x)


def _softplus(x):
    return jnp.maximum(x, 0.0) + jnp.log(1.0 + jnp.exp(-jnp.abs(x)))


def _dot(a, b):
    return jnp.dot(a.astype(BF16), b.astype(BF16), preferred_element_type=F32)


def _dot_nt(a, b):
    return lax.dot_general(a.astype(BF16), b.astype(BF16), (((1,), (1,)), ((), ())),
                           preferred_element_type=F32)


def _dot_tn(a, b):
    return lax.dot_general(a.astype(BF16), b.astype(BF16), (((0,), (0,)), ((), ())),
                           preferred_element_type=F32)


def _split(x):
    hi = x.astype(BF16)
    lo = (x - hi.astype(F32)).astype(BF16)
    return hi, lo


def _dot_exact_lhs(m, x):
    hi, lo = _split(x)
    return _dot(m, hi) + _dot(m, lo)


def _dot_tn_exact_rhs(x, m):
    hi, lo = _split(x)
    return _dot_tn(hi, m) + _dot_tn(lo, m)


def _iota(shape, dim):
    return lax.broadcasted_iota(jnp.int32, shape, dim)


def _cparams(sem):
    return pltpu.CompilerParams(dimension_semantics=sem, vmem_limit_bytes=VMEM_LIMIT)


def _mod_kernel(c_ref, w_ref, b_ref, o_ref):
    cs = _silu(c_ref[...])
    o_ref[0] = _dot(cs, w_ref[0]) + b_ref[0]


N_COND = 8


def _modulation(cond, w_mod, b_mod):
    return pl.pallas_call(
        _mod_kernel,
        grid=(DEPTH, 3),
        in_specs=[
            pl.BlockSpec((N_COND, D_MODEL), lambda l, j: (0, 0)),
            pl.BlockSpec((1, D_MODEL, D_MODEL), lambda l, j: (l, 0, j)),
            pl.BlockSpec((1, 1, D_MODEL), lambda l, j: (l, 0, j)),
        ],
        out_specs=pl.BlockSpec((1, N_COND, D_MODEL), lambda l, j: (l * 3 + j, 0, 0)),
        out_shape=jax.ShapeDtypeStruct((DEPTH * 3, N_COND, D_MODEL), F32),
        compiler_params=_cparams(("parallel", "parallel")),
        name="modulation",
    )(cond, w_mod, b_mod.reshape(DEPTH, 1, 3 * D_MODEL)).reshape(DEPTH * 3 * N_COND, 1, D_MODEL)


def _mod_row(layer, part, row):
    return (layer * 3 + part) * N_COND + row


def _mod_spec(layer, part, first_row, tiles_per_batch):
    base = _mod_row(layer, part, first_row)
    if tiles_per_batch is None:
        return pl.BlockSpec((1, 1, D_MODEL), lambda i: (base, 0, 0))
    return pl.BlockSpec((1, 1, D_MODEL), lambda i: (base + i // tiles_per_batch, 0, 0))


def _head_rms(x, bd, w):
    ss = _dot_exact_lhs_right(x * x, bd)
    return x * lax.rsqrt(ss * (1.0 / HD) + EPS) * w


def _dot_exact_lhs_right(x, m):
    hi, lo = _split(x)
    return _dot(hi, m) + _dot(lo, m)


def _rope(x, cos, sa, sb):
    return x * cos + pltpu.roll(x, LANE - ROT_AXIS // 2, axis=1) * sa + pltpu.roll(x, ROT_AXIS // 2, axis=1) * sb


def _short_conv_norm(zc, zp, zn, first, last, w_ref, xe_ref, o_ref):
    rows = zc.shape[0]
    xe_ref[0:HALO, :] = jnp.where(first, 0.0, zp)
    xe_ref[HALO:HALO + rows, :] = zc
    xe_ref[HALO + rows:, :] = jnp.where(last, 0.0, zn)
    qk_w = 2 * D_HEADS * D_DK

    def one_head(h):
        cols = slice(h * D_DK, (h + 1) * D_DK)
        acc = None
        for j in range(CONV_K):
            start = HALO - CONV_K // 2 + j
            term = xe_ref[start:start + rows, cols] * w_ref[j:j + 1, cols]
            acc = term if acc is None else acc + term
        y = _silu(acc)
        if h * D_DK < qk_w:
            y = y * lax.rsqrt(jnp.sum(y * y, axis=-1, keepdims=True) + EPS)
            if h < D_HEADS:
                y = y * (D_DK ** -0.5)
        o_ref[:, cols] = y

    return [functools.partial(one_head, h) for h in range(3 * D_HEADS)]


def _inproj_kernel(*refs, even, rope, kv_out, tiles_per_batch):
    x_ref, shift_ref, scale_ref, g_ref, w_ref = refs[:5]
    pos = 5
    if even:
        qn_ref, kn_ref, bd_ref = refs[pos:pos + 3]
        pos += 3
    else:
        xp_ref, xn_ref, cw_ref = refs[pos:pos + 3]
        pos += 3
    if rope:
        cos_ref, sa_ref, sb_ref = refs[pos:pos + 3]
        pos += 3
    o_ref = refs[pos]
    pos += 1
    if kv_out:
        ko_ref, vo_ref = refs[pos:pos + 2]
        pos += 2
    if not even:
        xe_ref = refs[pos]

    def modulated(x):
        y = x * lax.rsqrt(jnp.mean(x * x, axis=-1, keepdims=True) + EPS) * g_ref[...]
        return (y * (1.0 + scale_ref[0]) + shift_ref[0]).astype(BF16)

    h = modulated(x_ref[...])

    q_off, k_off = (E_Q, E_K) if even else (O_Q, O_K)
    width = E_W if even else O_W
    conv_w = 0 if even else 3 * D_HEADS * D_DK
    if not even:
        halo = modulated(jnp.concatenate([xp_ref[...], xn_ref[...]], axis=0))
        zc = jnp.dot(h, w_ref[:, 0:conv_w], preferred_element_type=F32)
        zh = jnp.dot(halo, w_ref[:, 0:conv_w], preferred_element_type=F32)

    def transform(z, is_q):
        if even:
            z = _head_rms(z, bd_ref[...], qn_ref[...] if is_q else kn_ref[...])
        if rope:
            z = _rope(z, cos_ref[...], sa_ref[...], sb_ref[...])
        return z

    wide = 2 * LANE
    pending = []
    spans = [(conv_w, width)]
    if even or rope or kv_out:
        starts = [q_off + i * wide for i in range(QW // wide)] + [k_off]
        pending = [(off, jnp.dot(h, w_ref[:, off:off + wide], preferred_element_type=F32)) for off in starts]
        spans, cur = [], conv_w
        for off in sorted(starts):
            if off > cur:
                spans.append((cur, off))
            cur = off + wide
        if cur < width:
            spans.append((cur, width))
    def project(s, e):
        o_ref[:, s:e] = jnp.dot(h, w_ref[:, s:e], preferred_element_type=F32)

    step = wide
    dots = [functools.partial(project, s, min(s + step, b)) for a, b in spans for s in range(a, b, step)]
    vpu = []
    if not even:
        i = pl.program_id(0)
        first = (i % tiles_per_batch) == 0
        last = (i % tiles_per_batch) == tiles_per_batch - 1
        vpu = _short_conv_norm(zc, zh[:HALO], zh[HALO:], first, last, cw_ref, xe_ref, o_ref)
    for k in range(max(len(dots), len(vpu))):
        if k < len(vpu):
            vpu[k]()
        if k < len(dots):
            dots[k]()
    for off, z in pending:
        if off == k_off:
            k_new = transform(z[:, :LANE], False)
            o_ref[:, off:off + LANE] = k_new
            o_ref[:, off + LANE:off + wide] = z[:, LANE:]
            if kv_out:
                ko_ref[...] = k_new
                vo_ref[...] = z[:, LANE:]
        else:
            for i in range(wide // LANE):
                o_ref[:, off + i * LANE:off + (i + 1) * LANE] = transform(z[:, i * LANE:(i + 1) * LANE], True)


def _inproj(x, mod, layer, cond_row, g_pre, w, *, even, rope_tabs=None, qn=None, kn=None, bd=None, conv_w=None,
            kv_out=False, rows_per_batch):
    T = x.shape[0]
    width = w.shape[1]
    tpb = rows_per_batch // ROW_TILE
    rope = rope_tabs is not None
    mod_tpb = None if cond_row is None else tpb
    ins = [x, mod, mod, g_pre.reshape(1, D_MODEL), w]
    specs = [
        pl.BlockSpec((ROW_TILE, D_MODEL), lambda i: (i, 0)),
        _mod_spec(layer, 0, cond_row or 0, mod_tpb),
        _mod_spec(layer, 1, cond_row or 0, mod_tpb),
        pl.BlockSpec((1, D_MODEL), lambda i: (0, 0)),
        pl.BlockSpec((D_MODEL, width), lambda i: (0, 0)),
    ]
    if even:
        ins += [qn, kn, bd]
        specs += [pl.BlockSpec((1, LANE), lambda i: (0, 0)),
                  pl.BlockSpec((1, LANE), lambda i: (0, 0)),
                  pl.BlockSpec((LANE, LANE), lambda i: (0, 0))]
        scratch = []
    else:
        hb = ROW_TILE // HALO
        n_hb = T // HALO
        cw = conv_w.shape[1]
        ins += [x, x, conv_w]
        specs += [pl.BlockSpec((HALO, D_MODEL), lambda i: (jnp.maximum(i * hb - 1, 0), 0)),
                  pl.BlockSpec((HALO, D_MODEL), lambda i: (jnp.minimum((i + 1) * hb, n_hb - 1), 0)),
                  pl.BlockSpec((CONV_K, cw), lambda i: (0, 0))]
        scratch = [pltpu.VMEM((ROW_TILE + 2 * HALO, cw), F32)]
    if rope:
        ins += list(rope_tabs)
        specs += [pl.BlockSpec((ROW_TILE, LANE), lambda i: (i % tpb, 0))] * 3
    out_specs = [pl.BlockSpec((ROW_TILE, width), lambda i: (i, 0))]
    out_shape = [jax.ShapeDtypeStruct((T, width), F32)]
    if kv_out:
        out_specs += [pl.BlockSpec((ROW_TILE, KVW), lambda i: (i, 0))] * 2
        out_shape += [jax.ShapeDtypeStruct((T, KVW), F32)] * 2
    outs = pl.pallas_call(
        functools.partial(_inproj_kernel, even=even, rope=rope, kv_out=kv_out, tiles_per_batch=tpb),
        grid=(T // ROW_TILE,),
        in_specs=specs,
        out_specs=out_specs,
        out_shape=out_shape,
        scratch_shapes=scratch,
        compiler_params=_cparams(("parallel",)),
        name="inproj_even" if even else "inproj_odd",
    )(*ins)
    return outs if kv_out else outs[0]


LOG2E = 1.4426950408889634
SCORE_SCALE = HD ** -0.5 * LOG2E


def _values_with_ones(v):
    lower = _iota(v.shape, 1) < HD
    return jnp.where(lower, v, 1.0).astype(BF16), jnp.where(lower, 1.0, v).astype(BF16)


def _pipelined_heads(scores, finish, lookahead, lower, g_ref, o_ref):
    heads = [(c, e) for c in range(QW // LANE) for e in range(N_KV)]
    queue = [scores(*h) for h in heads[:lookahead]]
    halves = {}
    for i, (c, e) in enumerate(heads):
        cur = queue.pop(0)
        if i + lookahead < len(heads):
            queue.append(scores(*heads[i + lookahead]))
        halves[e] = finish(c, e, *cur)
        if e == N_KV - 1:
            oc = jnp.where(lower, halves[0], halves[1])
            gated = oc * _silu(g_ref[:, c * LANE:(c + 1) * LANE])
            o_ref[:, c * LANE:(c + 1) * LANE] = gated.astype(o_ref.dtype)


def _attn_dense_kernel(*refs, has_ctx, has_sink, tq, lookahead):
    q_ref, k_ref, v_ref, g_ref = refs[:4]
    pos = 4
    if has_ctx:
        kc_ref, vc_ref = refs[pos:pos + 2]
        pos += 2
    if has_sink:
        sink_ref = refs[pos]
        pos += 1
    o_ref, kb_ref, vb0_ref, vb1_ref = refs[pos:pos + 4]

    @pl.when(pl.program_id(1) == 0)
    def _():
        kb_ref[...] = k_ref[...].astype(BF16)
        vb0_ref[...], vb1_ref[...] = _values_with_ones(v_ref[...])

    kb = kb_ref[...]
    vb = (vb0_ref[...], vb1_ref[...])
    if has_ctx:
        kcb = kc_ref[0].astype(BF16)
        vcb = _values_with_ones(vc_ref[0])
    lower = _iota((tq, LANE), 1) < HD

    def scores(c, e):
        qc = q_ref[:, c * LANE:(c + 1) * LANE] * SCORE_SCALE
        qe = jnp.where(lower if e == 0 else jnp.logical_not(lower), qc, 0.0).astype(BF16)
        return _dot_nt(qe, kb), (_dot_nt(qe, kcb) if has_ctx else None)

    def finish(c, e, s, sc):
        m = jnp.max(s, axis=-1, keepdims=True)
        if has_ctx:
            m = jnp.maximum(m, jnp.max(sc, axis=-1, keepdims=True))
        if has_sink:
            sk = sink_ref[c * N_KV + e] * LOG2E
            m = jnp.maximum(m, sk)
        o = _dot(jnp.exp2((s - m).astype(BF16)), vb[e])
        if has_ctx:
            o = o + _dot(jnp.exp2(sc - m), vcb[e])
        l = pltpu.roll(o, HD, axis=1)
        if has_sink:
            l = l + jnp.exp2(sk - m)
        return o / l

    _pipelined_heads(scores, finish, lookahead, lower, g_ref, o_ref)


def _attn_dense(z, *, n_batch, seq, tq, q_off, g_off, k_off, v_off, ctx=None, sink=None):
    T = z.shape[0]
    nq = seq // tq
    has_ctx = ctx is not None
    has_sink = sink is not None
    ins = [z, z, z, z]
    specs = [
        pl.BlockSpec((tq, QW), lambda b, i: (b * nq + i, q_off // QW)),
        pl.BlockSpec((seq, KVW), lambda b, i: (b, k_off // KVW)),
        pl.BlockSpec((seq, KVW), lambda b, i: (b, v_off // KVW)),
        pl.BlockSpec((tq, QW), lambda b, i: (b * nq + i, g_off // QW)),
    ]
    if has_ctx:
        ctx_k, ctx_v, layer = ctx
        ins += [ctx_k, ctx_v]
        specs += [pl.BlockSpec((1, None, ctx_k.shape[2], KVW), lambda b, i: (b, layer, 0, 0))] * 2
    if has_sink:
        ins.append(sink)
        specs.append(pl.BlockSpec(memory_space=pltpu.SMEM))
    return pl.pallas_call(
        functools.partial(_attn_dense_kernel, has_ctx=has_ctx, has_sink=has_sink, tq=tq,
                          lookahead=N_QH if seq <= 1024 else 2),
        grid=(n_batch, nq),
        in_specs=specs,
        out_specs=pl.BlockSpec((tq, QW), lambda b, i: (b * nq + i, 0)),
        out_shape=jax.ShapeDtypeStruct((T, QW), BF16),
        scratch_shapes=[pltpu.VMEM((seq, KVW), BF16)] * 3,
        compiler_params=_cparams(("arbitrary", "arbitrary")),
        name="attn_dense",
    )(*ins)


def _attn_band_kernel(q_ref, kp_ref, kc_ref, kn_ref, vp_ref, vc_ref, vn_ref, g_ref, kx_ref, vx_ref, sink_ref,
                      o_ref, *, seq):
    n = pl.program_id(1)
    tq = Q_BLOCK
    kcat = jnp.concatenate([kp_ref[...], kc_ref[...], kn_ref[...]], axis=0).astype(BF16)
    vcat = _values_with_ones(jnp.concatenate([vp_ref[...], vc_ref[...], vn_ref[...]], axis=0))
    kxb = kx_ref[0].astype(BF16)
    vxb = _values_with_ones(vx_ref[0])
    qpos = n * tq + _iota((tq, 3 * tq), 0)
    kpos = (n - 1) * tq + _iota((tq, 3 * tq), 1)
    valid = (jnp.abs(qpos - kpos) <= WINDOW) & (kpos >= 0) & (kpos < seq)
    lower = _iota((tq, LANE), 1) < HD

    def scores(c, e):
        qc = q_ref[:, c * LANE:(c + 1) * LANE] * SCORE_SCALE
        qe = jnp.where(lower if e == 0 else jnp.logical_not(lower), qc, 0.0).astype(BF16)
        return _dot_nt(qe, kcat), _dot_nt(qe, kxb)

    def finish(c, e, s, sx):
        s = jnp.where(valid, s, -jnp.inf)
        sk = sink_ref[c * N_KV + e] * LOG2E
        m = jnp.maximum(jnp.maximum(jnp.max(s, axis=-1, keepdims=True),
                                    jnp.max(sx, axis=-1, keepdims=True)), sk)
        o = _dot(jnp.exp2(s - m), vcat[e]) + _dot(jnp.exp2(sx - m), vxb[e])
        l = pltpu.roll(o, HD, axis=1) + jnp.exp2(sk - m)
        return o / l

    _pipelined_heads(scores, finish, N_QH, lower, g_ref, o_ref)


def _attn_band(z, ctx, sink, *, n_batch, seq):
    T = z.shape[0]
    nq = seq // Q_BLOCK
    ctx_k, ctx_v, layer = ctx
    ctx_spec = pl.BlockSpec((1, None, ctx_k.shape[2], KVW), lambda b, i: (b, layer, 0, 0))

    def kv_spec(off, delta):
        return pl.BlockSpec(
            (Q_BLOCK, KVW),
            lambda b, i: (b * nq + jnp.clip(i + delta, 0, nq - 1), off // KVW))

    specs = [
        pl.BlockSpec((Q_BLOCK, QW), lambda b, i: (b * nq + i, O_Q // QW)),
        kv_spec(O_K, -1), kv_spec(O_K, 0), kv_spec(O_K, 1),
        kv_spec(O_V, -1), kv_spec(O_V, 0), kv_spec(O_V, 1),
        pl.BlockSpec((Q_BLOCK, QW), lambda b, i: (b * nq + i, O_G // QW)),
        ctx_spec,
        ctx_spec,
        pl.BlockSpec(memory_space=pltpu.SMEM),
    ]
    return pl.pallas_call(
        functools.partial(_attn_band_kernel, seq=seq),
        grid=(n_batch, nq),
        in_specs=specs,
        out_specs=pl.BlockSpec((Q_BLOCK, QW), lambda b, i: (b * nq + i, 0)),
        out_shape=jax.ShapeDtypeStruct((T, QW), BF16),
        compiler_params=_cparams(("parallel", "parallel")),
        name="attn_band",
    )(z, z, z, z, z, z, z, z, ctx_k, ctx_v, sink)


def _gla_tile(dirs, n_chunk):
    rows = n_chunk * CHUNK
    shift = CHUNK.bit_length() - 1
    ii = _iota((rows, rows), 0)
    jj = _iota((rows, rows), 1)
    same = lax.shift_right_logical(ii, shift) == lax.shift_right_logical(jj, shift)
    sel = jnp.where(lax.shift_right_logical(_iota((rows, n_chunk * B_DV), 0), shift)
                    == lax.shift_right_logical(_iota((rows, n_chunk * B_DV), 1), B_DV.bit_length() - 1),
                    1.0, 0.0).astype(BF16)
    ci = _iota((CHUNK, CHUNK), 0)
    cj = _iota((CHUNK, CHUNK), 1)
    lower = _iota((CHUNK, LANE), 1) < B_DK
    n_pair = B_HEADS * B_DK // LANE
    per_pair = LANE // B_DK

    gs = []
    for (_, _, _, r_ref, up, bias, _, _, _) in dirs:
        gk = _dot(r_ref[...], up) + bias
        gs.append(-_softplus(-gk) * (1.0 / GLA_NORMALIZER))
    bs = []
    for g, d in zip(gs, dirs):
        tri = jnp.where(same & ((jj >= ii) if d[8] else (jj <= ii)), 1.0, 0.0).astype(BF16)
        bs.append(_dot_exact_lhs(tri, g))
    decargs = [_dot_tn_exact_rhs(g, sel) for g in gs]

    units = []
    for di, (q_ref, k_ref, v_ref, _, _, _, st_ref, o_ref, rev) in enumerate(dirs):
        b = bs[di]
        incl = (cj >= ci) if rev else (cj <= ci)
        order = range(n_chunk - 1, -1, -1) if rev else range(n_chunk)
        for c in order:
            r0 = c * CHUNK
            bc = b[r0:r0 + CHUNK]
            btot = bc[0:1, :] if rev else bc[CHUNK - 1:CHUNK, :]
            qt = q_ref[r0:r0 + CHUNK, :] * (B_DK ** -0.5) * jnp.exp(bc)
            kc = k_ref[r0:r0 + CHUNK, :]
            kt = kc * jnp.exp(-bc)
            kend = kc * jnp.exp(btot - bc)
            for p in range(n_pair):
                for e in range(per_pair):
                    h = p * per_pair + e
                    qm = jnp.where(lower if e == 0 else jnp.logical_not(lower), qt[:, p * LANE:(p + 1) * LANE], 0.0)
                    units.append(dict(di=di, c=c, p=p, e=e, incl=incl, qm=qm.astype(BF16),
                                      kt=kt[:, p * LANE:(p + 1) * LANE].astype(BF16),
                                      kend=kend[:, p * LANE:(p + 1) * LANE].astype(BF16),
                                      v=v_ref[r0:r0 + CHUNK, h * B_DV:(h + 1) * B_DV].astype(BF16)))
    for u in units:
        u["att"] = jnp.where(u["incl"], _dot_nt(u["qm"], u["kt"]), 0.0).astype(BF16)
    for u in units:
        u["upd"] = _dot_tn(u["kend"], u["v"])[u["e"] * B_DK:(u["e"] + 1) * B_DK]
    for di, d in enumerate(dirs):
        st = d[6][...]
        mine = [u for u in units if u["di"] == di]
        for k0 in range(0, len(mine), B_HEADS):
            grp = mine[k0:k0 + B_HEADS]
            c = grp[0]["c"]
            for u in grp:
                u["st"] = st[u["p"] * LANE:(u["p"] + 1) * LANE].astype(BF16)
            dec = jnp.exp(decargs[di][:, c * B_DV:(c + 1) * B_DV])
            st = dec * st + jnp.concatenate([u["upd"] for u in grp], axis=0)
        d[6][...] = st
    for u in units:
        u["qs"] = _dot(u["qm"], u["st"])
    for u in units:
        u["o"] = _dot(u["att"], u["v"]) + u["qs"]
    for di, d in enumerate(dirs):
        mine = [u for u in units if u["di"] == di]
        for k0 in range(0, len(mine), B_HEADS):
            grp = mine[k0:k0 + B_HEADS]
            r0 = grp[0]["c"] * CHUNK
            d[7][r0:r0 + CHUNK, :] = jnp.concatenate([u["o"] for u in grp], axis=1)


def _gla_kernel(*refs, has_s0, n_chunk):
    qf, kf, vf, rf, qb, kb, vb, rb, up_ref, bias_ref = refs[:10]
    pos = 10
    if has_s0:
        s0f, s0b = refs[pos:pos + 2]
        pos += 2
    of_ref, ob_ref, sf_ref, sb_ref, stf, stb = refs[pos:pos + 6]
    n = pl.program_id(1)

    @pl.when(n == 0)
    def _():
        if has_s0:
            stf[...] = s0f[0]
            stb[...] = s0b[0]
        else:
            stf[...] = jnp.zeros_like(stf)
            stb[...] = jnp.zeros_like(stb)

    _gla_tile([(qf, kf, vf, rf, up_ref[0], bias_ref[0], stf, of_ref, False),
               (qb, kb, vb, rb, up_ref[1], bias_ref[1], stb, ob_ref, True)], n_chunk)

    @pl.when(n == pl.num_programs(1) - 1)
    def _():
        sf_ref[0] = stf[...]
        sb_ref[0] = stb[...]


def _gla(z, up_ext, bias, s0, *, n_batch, seq):
    T = z.shape[0]
    nt = seq // REC_TILE
    has_s0 = s0 is not None
    sw = B_HEADS * B_DK

    def fwd(b, i):
        return b * nt + i

    def bwd(b, i):
        return b * nt + nt - 1 - i

    def specs_for(row):
        return [
            pl.BlockSpec((REC_TILE, sw), lambda b, i: (row(b, i), E_BQ // sw)),
            pl.BlockSpec((REC_TILE, sw), lambda b, i: (row(b, i), E_BK // sw)),
            pl.BlockSpec((REC_TILE, QW), lambda b, i: (row(b, i), E_BV // QW)),
            pl.BlockSpec((REC_TILE, LANE), lambda b, i: (row(b, i), E_R // LANE)),
        ]

    ins = [z] * 8 + [up_ext, bias]
    specs = specs_for(fwd) + specs_for(bwd) + [
        pl.BlockSpec((2, LANE, sw), lambda b, i: (0, 0, 0)),
        pl.BlockSpec((2, 1, sw), lambda b, i: (0, 0, 0)),
    ]
    if has_s0:
        states, layer = s0
        ins += [states, states]
        specs += [pl.BlockSpec((1, None, None, sw, B_DV), lambda b, i, d=d: (b, layer, d, 0, 0)) for d in range(2)]
    st_shape = jax.ShapeDtypeStruct((n_batch, sw, B_DV), F32)
    o_shape = jax.ShapeDtypeStruct((T, QW), F32)
    return pl.pallas_call(
        functools.partial(_gla_kernel, has_s0=has_s0, n_chunk=REC_TILE // CHUNK),
        grid=(n_batch, nt),
        in_specs=specs,
        out_specs=[
            pl.BlockSpec((REC_TILE, QW), lambda b, i: (fwd(b, i), 0)),
            pl.BlockSpec((REC_TILE, QW), lambda b, i: (bwd(b, i), 0)),
            pl.BlockSpec((1, sw, B_DV), lambda b, i: (b, 0, 0)),
            pl.BlockSpec((1, sw, B_DV), lambda b, i: (b, 0, 0)),
        ],
        out_shape=[o_shape, o_shape, st_shape, st_shape],
        scratch_shapes=[pltpu.VMEM((sw, B_DV), F32), pltpu.VMEM((sw, B_DV), F32)],
        compiler_params=_cparams(("arbitrary", "arbitrary")),
        name="gla",
    )(*ins)


def _dot3_stacked(lhs_list, rhs, expand=None):
    rh, rl = _split(rhs)
    if expand is not None:
        rh, rl = expand(rh), expand(rl)
    parts = [_split(x) for x in lhs_list]
    his = [p[0] for p in parts]
    los = [p[1] for p in parts]
    n = lhs_list[0].shape[0]
    k = len(lhs_list)
    r1 = _dot(jnp.concatenate(his + los, axis=0), rh)
    r2 = _dot(jnp.concatenate(his, axis=0) if k > 1 else his[0], rl)
    return [r1[i * n:(i + 1) * n] + (r1[(k + i) * n:(k + i + 1) * n] + r2[i * n:(i + 1) * n]) for i in range(k)]


def _delta_tile(dirs, alog, dtb, n_chunk):
    C = CHUNK
    W = D_HEADS * C
    hk = D_HEADS * D_DK
    sh = C.bit_length() - 1
    ii = _iota((C, C), 0)
    jj = _iota((C, C), 1)
    pi = _iota((C, W), 0)
    pl_ = _iota((C, W), 1)
    pj = pl_ & (C - 1)
    pblk = lax.shift_right_logical(pl_, sh)
    eye_p = jnp.where(pi == pj, 1.0, 0.0)
    bdm = (lax.shift_right_logical(_iota((W, W), 0), sh) == lax.shift_right_logical(_iota((W, W), 1), sh))
    bdk = (lax.shift_right_logical(_iota((W, hk), 0), sh)
           == lax.shift_right_logical(_iota((W, hk), 1), D_DK.bit_length() - 1))
    ones_cc = jnp.ones((C, C), BF16)

    def pack(cols):
        out = jnp.broadcast_to(cols[D_HEADS - 1], (C, W))
        for h in range(D_HEADS - 2, -1, -1):
            out = jnp.where(pblk == h, jnp.broadcast_to(cols[h], (C, W)), out)
        return out

    def wide(cols):
        return jnp.concatenate([jnp.broadcast_to(c, (C, D_DK)) for c in cols], axis=1)

    def block_diag(x):
        return jnp.where(bdm, jnp.concatenate([x] * D_HEADS, axis=0), jnp.zeros((), x.dtype))

    units = []
    for d, (x_ref, ab_ref, st_ref, o_ref, rev) in enumerate(dirs):
        incl = (jj >= ii) if rev else (jj <= ii)
        tri = jnp.where(incl, 1.0, 0.0).astype(BF16)
        incl_p = (pj >= pi) if rev else (pj <= pi)
        strict_p = (pj > pi) if rev else (pj < pi)
        before_p = (pj <= pi) if rev else (pj >= pi)
        order = range(n_chunk - 1, -1, -1) if rev else range(n_chunk)
        for c in order:
            r0 = c * C
            ab = ab_ref[r0:r0 + C, :]
            gall = -jnp.exp(alog) * _softplus(ab + dtb)
            beta_all = _sigmoid(ab)
            gc_all = _dot_exact_lhs(tri, gall)
            ig = [d * D_HEADS + h for h in range(D_HEADS)]
            gcols = [gc_all[:, i:i + 1] for i in ig]
            g_p = pack([gall[:, i:i + 1] for i in ig])
            gc_p = pack(gcols)
            gr_p = _dot_exact_lhs(ones_cc, jnp.where(before_p, g_p, 0.0))
            decay = jnp.exp(jnp.where(incl_p, gc_p - gr_p, -jnp.inf))
            gc_w = wide(gcols)
            beta_w = wide([beta_all[:, 2 * D_HEADS + i:2 * D_HEADS + i + 1] for i in ig])
            glast_w = gc_w[0:1, :] if rev else gc_w[C - 1:C, :]
            egc_w = jnp.exp(gc_w)
            q_all = x_ref[r0:r0 + C, 0:hk]
            k_all = x_ref[r0:r0 + C, hk:2 * hk]
            v_all = x_ref[r0:r0 + C, 2 * hk:3 * hk]
            kb_all = k_all * beta_w
            k_bd = jnp.where(bdk, jnp.concatenate([k_all] * D_HEADS, axis=0), 0.0)
            r = _dot_nt(jnp.concatenate([kb_all, q_all], axis=0), k_bd)
            p = -jnp.where(strict_p, r[:C] * decay, 0.0)
            vb_all = v_all * beta_w
            kbe_all = kb_all * egc_w
            rhs = jnp.concatenate(
                [jnp.concatenate([vb_all[:, h * D_DV:(h + 1) * D_DV], kbe_all[:, h * D_DK:(h + 1) * D_DK]], axis=1)
                 for h in range(D_HEADS)], axis=0)
            units.append(dict(d=d, r0=r0, st_ref=st_ref, o_ref=o_ref, p=p, acc=eye_p + p, att=r[C:] * decay,
                              rhs=rhs, qdec=q_all * egc_w, kdec=k_all * jnp.exp(glast_w - gc_w),
                              eglast=jnp.exp(glast_w)))
    per_pair = LANE // C
    half = lax.shift_right_logical(_iota((C, LANE), 1), sh)

    def square_first(us):
        for u in us:
            u["p"] = _dot3_stacked([u["p"]], u["p"], block_diag)[0]

    def square_and_extend(us):
        for u in us:
            pp, ap = _dot3_stacked([u["p"], u["acc"]], u["p"], block_diag)
            u["acc"] = u["acc"] + ap
            u["p"] = pp

    def extend_last(us):
        for u in us:
            u["acc"] = u["acc"] + _dot3_stacked([u["acc"]], u["p"], block_diag)[0]

    def solve(us):
        for u in us:
            u["sol"] = []
            for c in range(D_HEADS // per_pair):
                lanes = slice(c * LANE, (c + 1) * LANE)
                lhs = [jnp.where(half == e, u["acc"][:, lanes], 0.0) for e in range(per_pair)]
                u["sol"] += _dot3_stacked(lhs, u["rhs"][c * LANE:(c + 1) * LANE])

    n_mid = 0
    m = 2
    while 2 * m < C:
        n_mid += 1
        m *= 2
    local_stages = [square_first] + [square_and_extend] * n_mid + [extend_last, solve]

    def read_state(cur):
        for u in cur:
            u["s"] = [u["st_ref"][h] for h in range(D_HEADS)]
            u["vnew"], u["qs"] = [], []
            for h in range(D_HEADS):
                lhs = jnp.concatenate([u["sol"][h][:, D_DV:], u["qdec"][:, h * D_DK:(h + 1) * D_DK]], axis=0)
                r = _dot(lhs, u["s"][h])
                u["vnew"].append(u["sol"][h][:, :D_DV] - r[:C])
                u["qs"].append(r[C:])

    def write_state(cur):
        for u in cur:
            o_pairs = []
            for c in range(D_HEADS // per_pair):
                lanes = slice(c * LANE, (c + 1) * LANE)
                att_rows = jnp.concatenate(
                    [jnp.where(half == e, u["att"][:, lanes], 0.0) for e in range(per_pair)],
                    axis=0)
                o_pairs.append(_dot(att_rows, jnp.concatenate(u["vnew"][c * per_pair:(c + 1) * per_pair], axis=0)))
            o_all = jnp.concatenate(o_pairs, axis=0)
            outs = []
            for h in range(D_HEADS):
                outs.append(u["qs"][h] + o_all[h * C:(h + 1) * C])
                u["st_ref"][h] = (u["s"][h] * u["eglast"][:, h * D_DK:(h + 1) * D_DK]
                                  + _dot_tn(u["kdec"][:, h * D_DK:(h + 1) * D_DK], u["vnew"][h]))
            u["o_ref"][u["r0"]:u["r0"] + C, :] = jnp.concatenate(outs, axis=1)

    per_dir = [[u for u in units if u["d"] == d] for d in range(len(dirs))]
    steps = [[lst[i] for lst in per_dir] for i in range(n_chunk)]
    n_first = n_chunk // 2
    for stage in local_stages:
        stage([u for cur in steps[:n_first] for u in cur])
    pending = [functools.partial(f, cur) for cur in steps[:n_first] for f in (read_state, write_state)]
    for stage in local_stages:
        stage([u for cur in steps[n_first:] for u in cur])
        if pending:
            pending.pop(0)()
    for f in pending:
        f()
    for cur in steps[n_first:]:
        read_state(cur)
        write_state(cur)


def _delta_kernel(*refs, has_s0, n_chunk):
    xf, af, xb, ab_, alog_ref, dtb_ref = refs[:6]
    pos = 6
    if has_s0:
        s0f, s0b = refs[pos:pos + 2]
        pos += 2
    of_ref, ob_ref, sf_ref, sb_ref, stf, stb = refs[pos:pos + 6]
    n = pl.program_id(1)

    @pl.when(n == 0)
    def _():
        if has_s0:
            stf[...] = s0f[0]
            stb[...] = s0b[0]
        else:
            stf[...] = jnp.zeros_like(stf)
            stb[...] = jnp.zeros_like(stb)

    _delta_tile([(xf, af, stf, of_ref, False), (xb, ab_, stb, ob_ref, True)],
                alog_ref[...], dtb_ref[...], n_chunk)

    @pl.when(n == pl.num_programs(1) - 1)
    def _():
        sf_ref[0] = stf[...]
        sb_ref[0] = stb[...]


def _delta(z, alog_row, dtb_row, s0, *, n_batch, seq):
    T = z.shape[0]
    nt = seq // REC_TILE
    has_s0 = s0 is not None
    cw = 3 * D_HEADS * D_DK
    assert O_DQKV == 0 and ROW_TILE == REC_TILE

    def fwd(b, i):
        return b * nt + i

    def bwd(b, i):
        return b * nt + nt - 1 - i

    ins = [z, z, z, z, alog_row, dtb_row]
    specs = [
        pl.BlockSpec((REC_TILE, cw), lambda b, i: (fwd(b, i), 0)),
        pl.BlockSpec((REC_TILE, LANE), lambda b, i: (fwd(b, i), O_AB // LANE)),
        pl.BlockSpec((REC_TILE, cw), lambda b, i: (bwd(b, i), 0)),
        pl.BlockSpec((REC_TILE, LANE), lambda b, i: (bwd(b, i), O_AB // LANE)),
        pl.BlockSpec((1, LANE), lambda b, i: (0, 0)),
        pl.BlockSpec((1, LANE), lambda b, i: (0, 0)),
    ]
    st_block = (1, D_HEADS, D_DK, D_DV)
    if has_s0:
        states, layer = s0
        ins += [states, states]
        specs += [pl.BlockSpec((1, None, None, D_HEADS, D_DK, D_DV), lambda b, i, d=d: (b, layer, d, 0, 0, 0))
                  for d in range(2)]
    st_shape = jax.ShapeDtypeStruct((n_batch, D_HEADS, D_DK, D_DV), F32)
    o_shape = jax.ShapeDtypeStruct((T, QW), F32)
    return pl.pallas_call(
        functools.partial(_delta_kernel, has_s0=has_s0, n_chunk=REC_TILE // CHUNK),
        grid=(n_batch, nt),
        in_specs=specs,
        out_specs=[
            pl.BlockSpec((REC_TILE, QW), lambda b, i: (fwd(b, i), 0)),
            pl.BlockSpec((REC_TILE, QW), lambda b, i: (bwd(b, i), 0)),
            pl.BlockSpec(st_block, lambda b, i: (b, 0, 0, 0)),
            pl.BlockSpec(st_block, lambda b, i: (b, 0, 0, 0)),
        ],
        out_shape=[o_shape, o_shape, st_shape, st_shape],
        scratch_shapes=[pltpu.VMEM((D_HEADS, D_DK, D_DV), F32), pltpu.VMEM((D_HEADS, D_DK, D_DV), F32)],
        compiler_params=_cparams(("arbitrary", "arbitrary")),
        name="delta",
    )(*ins)


def _outproj_kernel(ya_ref, of_ref, ob_ref, zg_ref, on_ref, w_ref, x_ref, gate_ref, gp_ref, o_ref):
    ob = of_ref[...] + ob_ref[...]
    zg = zg_ref[...]
    on = on_ref[...]
    parts = []
    for h in range(QW // LANE):
        seg = ob[:, h * LANE:(h + 1) * LANE]
        nrm = seg * lax.rsqrt(jnp.mean(seg * seg, axis=-1, keepdims=True) + EPS) * on
        parts.append(nrm * _silu(zg[:, h * LANE:(h + 1) * LANE]))
    yb = jnp.concatenate(parts, axis=1)
    out = _dot(ya_ref[...], w_ref[0:QW, :]) + _dot(yb, w_ref[QW:, :])
    post = out * lax.rsqrt(jnp.mean(out * out, axis=-1, keepdims=True) + EPS) * gp_ref[...]
    o_ref[...] = x_ref[...] + gate_ref[0] * post


def _outproj(ya, o_f, o_b, z, g_off, onorm, w_out, x, mod, layer, cond_row, g_post, *, rows_per_batch):
    T = x.shape[0]
    row = lambda i: (i, 0)
    gate_spec = _mod_spec(layer, 2, cond_row or 0, None if cond_row is None else rows_per_batch // OUT_TILE)
    return pl.pallas_call(
        _outproj_kernel,
        grid=(T // OUT_TILE,),
        in_specs=[
            pl.BlockSpec((OUT_TILE, QW), row),
            pl.BlockSpec((OUT_TILE, QW), row),
            pl.BlockSpec((OUT_TILE, QW), row),
            pl.BlockSpec((OUT_TILE, QW), lambda i: (i, g_off // QW)),
            pl.BlockSpec((1, LANE), lambda i: (0, 0)),
            pl.BlockSpec((2 * QW, D_MODEL), lambda i: (0, 0)),
            pl.BlockSpec((OUT_TILE, D_MODEL), row),
            gate_spec,
            pl.BlockSpec((1, D_MODEL), lambda i: (0, 0)),
        ],
        out_specs=pl.BlockSpec((OUT_TILE, D_MODEL), row),
        out_shape=jax.ShapeDtypeStruct((T, D_MODEL), F32),
        compiler_params=_cparams(("parallel",)),
        name="outproj",
    )(ya, o_f, o_b, z, onorm.reshape(1, LANE), w_out, x, mod, g_post.reshape(1, D_MODEL))


def _rope_tables(seq):
    n_rows = seq // GRID_W
    rows = jnp.repeat(jnp.arange(n_rows, dtype=F32), GRID_W)
    cols = jnp.tile(jnp.arange(GRID_W, dtype=F32), n_rows)
    inv = jnp.power(ROPE_THETA, jnp.arange(ROT_AXIS // 2, dtype=F32) * (-2.0 / ROT_AXIS))
    ang_r = rows[:, None] * inv[None, :]
    ang_c = cols[:, None] * inv[None, :]
    zero = jnp.zeros_like(ang_r)
    cos = jnp.concatenate([jnp.cos(ang_r)] * 2 + [jnp.cos(ang_c)] * 2, axis=1)
    sa = jnp.concatenate([-jnp.sin(ang_r), zero, -jnp.sin(ang_c), zero], axis=1)
    sb = jnp.concatenate([zero, jnp.sin(ang_r), zero, jnp.sin(ang_c)], axis=1)
    return tuple(jnp.tile(t, (1, LANE // HD)) for t in (cos, sa, sb))


def _regroup_cols(w):
    g = N_QH // N_KV
    lead = w.shape[:-1]
    return jnp.swapaxes(w.reshape(lead + (N_KV, g, HD)), -3, -2).reshape(lead + (QW,))


def _regroup_rows(w):
    g = N_QH // N_KV
    return jnp.swapaxes(w.reshape(w.shape[:-2] + (N_KV, g, HD, w.shape[-1])), -4, -3).reshape(w.shape)


def _prep_even_weights(w_in, w_out, gk_up):
    n = w_in.shape[0]
    w_in = w_in.astype(BF16)
    a_q, a_k, a_v, a_g, b_q, b_k, b_v, b_r, b_g = jnp.split(
        w_in, [512, 640, 768, 1280, 1536, 1792, 2304, 2336], axis=2)
    pad = jnp.zeros((n, D_MODEL, LANE - 2 * B_RANK), BF16)
    w = jnp.concatenate([_regroup_cols(a_q), _regroup_cols(a_g), b_v, b_g, b_q, b_k, a_k, a_v, b_r, pad], axis=2)
    w_out = w_out.astype(BF16)
    wo = jnp.concatenate([_regroup_rows(w_out[:, :QW]), w_out[:, QW:]], axis=1)
    zeros = jnp.zeros((n, B_RANK, B_HEADS * B_DK), BF16)
    tail = jnp.zeros((n, LANE - 2 * B_RANK, B_HEADS * B_DK), BF16)
    gk_up = gk_up.astype(BF16)
    up = jnp.stack([jnp.concatenate([gk_up[:, 0], zeros, tail], axis=1),
                    jnp.concatenate([zeros, gk_up[:, 1], tail], axis=1)], axis=1)
    return w, wo, up


def _prep_odd_weights(w_in, w_out):
    n = w_in.shape[0]
    w_in = w_in.astype(BF16)
    c_q, c_k, c_v, c_g, d_q, d_k, d_v, d_a, d_b, d_g = jnp.split(
        w_in, [512, 640, 768, 1280, 1792, 2304, 2816, 2824, 2832], axis=2)
    pad = jnp.zeros((n, D_MODEL, LANE - 4 * D_HEADS), BF16)
    w = jnp.concatenate([d_q, d_k, d_v, _regroup_cols(c_q), _regroup_cols(c_g), d_g, c_k, c_v, d_a, d_b, pad],
                        axis=2)
    w_out = w_out.astype(BF16)
    wo = jnp.concatenate([_regroup_rows(w_out[:, :QW]), w_out[:, QW:]], axis=1)
    return w, wo


def _lane_row(x):
    flat = x.reshape(-1).astype(F32)
    return jnp.zeros((1, LANE), F32).at[0, :flat.shape[0]].set(flat)


def kernel(x_prompt, x_sample, c, cache_attn_k, cache_attn_v, state_gla, cache_swa_k, cache_swa_v, state_delta,
           c_ctx, w_mod, b_mod, g_pre, g_post, w_in_even, w_out_even, qnorm_a, knorm_a, gla_gk_up, gla_gk_bias,
           gla_onorm, w_in_odd, w_out_odd, sink_c, conv_d, a_log_d, dt_bias_d, delta_onorm):
    n_p, seq_p, _ = x_prompt.shape
    n_s, seq_s, _ = x_sample.shape
    n_ctx = cache_attn_k.shape[2]
    assert (n_p * seq_p) % OUT_TILE == 0 and seq_s % OUT_TILE == 0 and seq_p % ROW_TILE == 0
    assert seq_s % ROW_TILE == 0 and seq_p % REC_TILE == 0 and seq_s % REC_TILE == 0
    assert n_s + 1 <= N_COND

    cond = jnp.zeros((N_COND, D_MODEL), F32).at[0].set(c_ctx).at[1:1 + n_s].set(c)
    mod = _modulation(cond, w_mod, b_mod)
    rope = _rope_tables(seq_s)
    bd = jnp.asarray(np.kron(np.eye(LANE // HD), np.ones((HD, HD))), BF16)
    w_even, wo_even, up_even = _prep_even_weights(w_in_even, w_out_even, gla_gk_up)
    w_odd, wo_odd = _prep_odd_weights(w_in_odd, w_out_odd)
    sinks = jnp.swapaxes(sink_c.astype(F32).reshape(-1, N_KV, N_QH // N_KV), 1, 2).reshape(-1, N_QH)
    ctx_attn_k, ctx_attn_v, ctx_swa_k, ctx_swa_v = [
        a.reshape(a.shape[0], a.shape[1], n_ctx, KVW) for a in (cache_attn_k, cache_attn_v, cache_swa_k, cache_swa_v)]
    gla_states = state_gla.reshape(n_s, -1, 2, B_HEADS * B_DK, B_DV)

    y_p = x_prompt.reshape(n_p * seq_p, D_MODEL)
    y_s = x_sample.reshape(n_s * seq_s, D_MODEL)
    attn_k, attn_v, gla_s, swa_k, swa_v, delta_s = [], [], [], [], [], []
    for l in range(DEPTH):
        i = l // 2
        if l % 2 == 0:
            w, wo, up = w_even[i], wo_even[i], up_even[i]
            qn = jnp.tile(qnorm_a[i], LANE // HD).reshape(1, LANE)
            kn = jnp.tile(knorm_a[i], LANE // HD).reshape(1, LANE)
            bias = gla_gk_bias[i].reshape(2, 1, B_HEADS * B_DK)
            common = dict(even=True, qn=qn, kn=kn, bd=bd)
            z_p, k_new, v_new = _inproj(y_p, mod, l, None, g_pre[l], w, rows_per_batch=seq_p, kv_out=True, **common)
            z_s = _inproj(y_s, mod, l, 1, g_pre[l], w, rows_per_batch=seq_s, rope_tabs=rope, **common)
            offs = dict(q_off=E_Q, g_off=E_G, k_off=E_K, v_off=E_V)
            ya_p = _attn_dense(z_p, n_batch=n_p, seq=seq_p, tq=seq_p, **offs)
            ctx = (ctx_attn_k, ctx_attn_v, i)
            ya_s = _attn_dense(z_s, n_batch=n_s, seq=seq_s, tq=256, ctx=ctx, **offs)
            of_p, ob_p, sf_p, sb_p = _gla(z_p, up, bias, None, n_batch=n_p, seq=seq_p)
            of_s, ob_s, _, _ = _gla(z_s, up, bias, (gla_states, i), n_batch=n_s, seq=seq_s)
            attn_k.append(k_new.reshape(n_p, seq_p, N_KV, HD))
            attn_v.append(v_new.reshape(n_p, seq_p, N_KV, HD))
            gla_s.append(jnp.stack([sf_p, sb_p], axis=1).reshape(n_p, 2, B_HEADS, B_DK, B_DV))
            g_off, onorm = E_BG, gla_onorm[i]
        else:
            w, wo = w_odd[i], wo_odd[i]
            z_p, k_new, v_new = _inproj(y_p, mod, l, None, g_pre[l], w, even=False, conv_w=conv_d[i],
                                        kv_out=True, rows_per_batch=seq_p)
            z_s = _inproj(y_s, mod, l, 1, g_pre[l], w, even=False, conv_w=conv_d[i], rows_per_batch=seq_s,
                          rope_tabs=rope)
            sink = sinks[i]
            ya_p = _attn_dense(z_p, n_batch=n_p, seq=seq_p, tq=seq_p, q_off=O_Q, g_off=O_G, k_off=O_K,
                               v_off=O_V, sink=sink)
            ctx = (ctx_swa_k, ctx_swa_v, i)
            ya_s = _attn_band(z_s, ctx, sink, n_batch=n_s, seq=seq_s)
            alog_row = _lane_row(a_log_d[i])
            dtb_row = _lane_row(dt_bias_d[i])
            of_p, ob_p, sf_p, sb_p = _delta(z_p, alog_row, dtb_row, None, n_batch=n_p, seq=seq_p)
            of_s, ob_s, _, _ = _delta(z_s, alog_row, dtb_row, (state_delta, i), n_batch=n_s, seq=seq_s)
            swa_k.append(k_new.reshape(n_p, seq_p, N_KV, HD))
            swa_v.append(v_new.reshape(n_p, seq_p, N_KV, HD))
            delta_s.append(jnp.stack([sf_p, sb_p], axis=1))
            g_off, onorm = O_DG, delta_onorm[i]
        y_p = _outproj(ya_p, of_p, ob_p, z_p, g_off, onorm, wo, y_p, mod, l, None, g_post[l], rows_per_batch=seq_p)
        y_s = _outproj(ya_s, of_s, ob_s, z_s, g_off, onorm, wo, y_s, mod, l, 1, g_post[l], rows_per_batch=seq_s)
    return (y_p.reshape(n_p, seq_p, D_MODEL), y_s.reshape(n_s, seq_s, D_MODEL),
            jnp.stack(attn_k, axis=1), jnp.stack(attn_v, axis=1), jnp.stack(gla_s, axis=1),
            jnp.stack(swa_k, axis=1), jnp.stack(swa_v, axis=1), jnp.stack(delta_s, axis=1))
```

```python
import functools

import numpy as np
import jax
import jax.numpy as jnp
from jax import lax
from jax.experimental import pallas as pl
from jax.experimental.pallas import tpu as pltpu

F32 = jnp.float32
BF16 = jnp.bfloat16

D_MODEL = 1024
DEPTH = 4
GRID_W = 64
HD = 64
N_QH = 8
N_KV = 2
QW = N_QH * HD
KVW = N_KV * HD
B_HEADS = 4
B_DK = 64
B_DV = 128
B_RANK = 16
GLA_NORMALIZER = 16.0
D_HEADS = 4
D_DK = 128
D_DV = 128
CONV_K = 5
WINDOW = 128
Q_BLOCK = 128
CHUNK = 64
ROPE_THETA = 10000.0
ROT_AXIS = HD // 2
EPS = 1e-6
LANE = 128
HALO = 8

ROW_TILE = 256
OUT_TILE = 1024
REC_TILE = 256
VMEM_LIMIT = 48 * 1024 * 1024

E_Q, E_G, E_BV, E_BG, E_BQ, E_BK, E_K, E_V, E_R, E_W = 0, 512, 1024, 1536, 2048, 2304, 2560, 2688, 2816, 2944
O_DQKV, O_Q, O_G, O_DG, O_K, O_V, O_AB, O_W = 0, 1536, 2048, 2560, 3072, 3200, 3328, 3456
assert E_V == E_K + KVW and O_V == O_K + KVW


def _sigmoid(x):
    return 1.0 / (1.0 + jnp.exp(-x))


def _silu(x):
    return x * _sigmoid(x)


def _softplus(x):
    return jnp.maximum(x, 0.0) + jnp.log(1.0 + jnp.exp(-jnp.abs(x)))


def _dot(a, b):
    return jnp.dot(a.astype(BF16), b.astype(BF16), preferred_element_type=F32)


def _dot_nt(a, b):
    return lax.dot_general(a.astype(BF16), b.astype(BF16), (((1,), (1,)), ((), ())),
                           preferred_element_type=F32)


def _dot_tn(a, b):
    return lax.dot_general(a.astype(BF16), b.astype(BF16), (((0,), (0,)), ((), ())),
                           preferred_element_type=F32)


def _split(x):
    hi = x.astype(BF16)
    lo = (x - hi.astype(F32)).astype(BF16)
    return hi, lo


def _dot_exact_lhs(m, x):
    hi, lo = _split(x)
    return _dot(m, hi) + _dot(m, lo)


def _dot_tn_exact_rhs(x, m):
    hi, lo = _split(x)
    return _dot_tn(hi, m) + _dot_tn(lo, m)


def _iota(shape, dim):
    return lax.broadcasted_iota(jnp.int32, shape, dim)


def _cparams(sem, fuse_inputs=None):
    return pltpu.CompilerParams(dimension_semantics=sem, vmem_limit_bytes=VMEM_LIMIT,
                                allow_input_fusion=fuse_inputs)


def _mod_kernel(c_ref, w_ref, b_ref, o_ref):
    cs = _silu(c_ref[...])
    o_ref[0] = _dot(cs, w_ref[0]) + b_ref[0]


N_COND = 8


def _modulation(cond, w_mod, b_mod):
    return pl.pallas_call(
        _mod_kernel,
        grid=(DEPTH, 3),
        in_specs=[
            pl.BlockSpec((N_COND, D_MODEL), lambda l, j: (0, 0)),
            pl.BlockSpec((1, D_MODEL, D_MODEL), lambda l, j: (l, 0, j)),
            pl.BlockSpec((1, 1, D_MODEL), lambda l, j: (l, 0, j)),
        ],
        out_specs=pl.BlockSpec((1, N_COND, D_MODEL), lambda l, j: (l * 3 + j, 0, 0)),
        out_shape=jax.ShapeDtypeStruct((DEPTH * 3, N_COND, D_MODEL), F32),
        compiler_params=_cparams(("parallel", "parallel")),
        name="modulation",
    )(cond, w_mod, b_mod.reshape(DEPTH, 1, 3 * D_MODEL)).reshape(DEPTH * 3 * N_COND, 1, D_MODEL)


def _mod_row(layer, part, row):
    return (layer * 3 + part) * N_COND + row


def _mod_spec(layer, part, first_row, tiles_per_batch):
    base = _mod_row(layer, part, first_row)
    if tiles_per_batch is None:
        return pl.BlockSpec((1, 1, D_MODEL), lambda i: (base, 0, 0))
    return pl.BlockSpec((1, 1, D_MODEL), lambda i: (base + i // tiles_per_batch, 0, 0))


def _head_rms(x, bd, w):
    ss = _dot_exact_lhs_right(x * x, bd)
    return x * lax.rsqrt(ss * (1.0 / HD) + EPS) * w


def _dot_exact_lhs_right(x, m):
    hi, lo = _split(x)
    return _dot(hi, m) + _dot(lo, m)


def _rope(x, cos, sa, sb):
    return x * cos + pltpu.roll(x, LANE - ROT_AXIS // 2, axis=1) * sa + pltpu.roll(x, ROT_AXIS // 2, axis=1) * sb


def _short_conv_norm(zc, zp, zn, first, last, w_ref, xe_ref, o_ref):
    rows = zc.shape[0]
    xe_ref[0:HALO, :] = jnp.where(first, 0.0, zp)
    xe_ref[HALO:HALO + rows, :] = zc
    xe_ref[HALO + rows:, :] = jnp.where(last, 0.0, zn)
    qk_w = 2 * D_HEADS * D_DK

    def one_head(h):
        cols = slice(h * D_DK, (h + 1) * D_DK)
        acc = None
        for j in range(CONV_K):
            start = HALO - CONV_K // 2 + j
            term = xe_ref[start:start + rows, cols] * w_ref[j:j + 1, cols]
            acc = term if acc is None else acc + term
        y = _silu(acc)
        if h * D_DK < qk_w:
            y = y * lax.rsqrt(jnp.sum(y * y, axis=-1, keepdims=True) + EPS)
            if h < D_HEADS:
                y = y * (D_DK ** -0.5)
        o_ref[:, cols] = y

    return [functools.partial(one_head, h) for h in range(3 * D_HEADS)]


def _inproj_kernel(*refs, even, rope, kv_out, tiles_per_batch):
    x_ref, shift_ref, scale_ref, g_ref, w_ref = refs[:5]
    pos = 5
    if even:
        qn_ref, kn_ref, bd_ref = refs[pos:pos + 3]
        pos += 3
    else:
        xp_ref, xn_ref, cw_ref = refs[pos:pos + 3]
        pos += 3
    if rope:
        cos_ref, sa_ref, sb_ref = refs[pos:pos + 3]
        pos += 3
    o_ref = refs[pos]
    pos += 1
    if kv_out:
        ko_ref, vo_ref = refs[pos:pos + 2]
        pos += 2
    if not even:
        xe_ref = refs[pos]

    def modulated(x):
        y = x * lax.rsqrt(jnp.mean(x * x, axis=-1, keepdims=True) + EPS) * g_ref[...]
        return (y * (1.0 + scale_ref[0]) + shift_ref[0]).astype(BF16)

    h = modulated(x_ref[...])

    q_off, k_off = (E_Q, E_K) if even else (O_Q, O_K)
    width = E_W if even else O_W
    conv_w = 0 if even else 3 * D_HEADS * D_DK
    if not even:
        halo = modulated(jnp.concatenate([xp_ref[...], xn_ref[...]], axis=0))
        zc = jnp.dot(h, w_ref[:, 0:conv_w], preferred_element_type=F32)
        zh = jnp.dot(halo, w_ref[:, 0:conv_w], preferred_element_type=F32)

    def transform(z, is_q):
        if even:
            z = _head_rms(z, bd_ref[...], qn_ref[...] if is_q else kn_ref[...])
        if rope:
            z = _rope(z, cos_ref[...], sa_ref[...], sb_ref[...])
        return z

    wide = 2 * LANE
    pending = []
    spans = [(conv_w, width)]
    if even or rope or kv_out:
        starts = [q_off + i * wide for i in range(QW // wide)] + [k_off]
        pending = [(off, jnp.dot(h, w_ref[:, off:off + wide], preferred_element_type=F32)) for off in starts]
        spans, cur = [], conv_w
        for off in sorted(starts):
            if off > cur:
                spans.append((cur, off))
            cur = off + wide
        if cur < width:
            spans.append((cur, width))
    def project(s, e):
        o_ref[:, s:e] = jnp.dot(h, w_ref[:, s:e], preferred_element_type=F32)

    step = wide
    dots = [functools.partial(project, s, min(s + step, b)) for a, b in spans for s in range(a, b, step)]
    vpu = []
    if not even:
        i = pl.program_id(0)
        first = (i % tiles_per_batch) == 0
        last = (i % tiles_per_batch) == tiles_per_batch - 1
        vpu = _short_conv_norm(zc, zh[:HALO], zh[HALO:], first, last, cw_ref, xe_ref, o_ref)
    for k in range(max(len(dots), len(vpu))):
        if k < len(vpu):
            vpu[k]()
        if k < len(dots):
            dots[k]()
    for off, z in pending:
        if off == k_off:
            k_new = transform(z[:, :LANE], False)
            o_ref[:, off:off + LANE] = k_new
            o_ref[:, off + LANE:off + wide] = z[:, LANE:]
            if kv_out:
                ko_ref[...] = k_new
                vo_ref[...] = z[:, LANE:]
        else:
            for i in range(wide // LANE):
                o_ref[:, off + i * LANE:off + (i + 1) * LANE] = transform(z[:, i * LANE:(i + 1) * LANE], True)


def _inproj(x, mod, layer, cond_row, g_pre, w, *, even, rope_tabs=None, qn=None, kn=None, bd=None, conv_w=None,
            kv_out=False, rows_per_batch):
    T = x.shape[0]
    width = w.shape[1]
    tpb = rows_per_batch // ROW_TILE
    rope = rope_tabs is not None
    mod_tpb = None if cond_row is None else tpb
    ins = [x, mod, mod, g_pre.reshape(1, D_MODEL), w]
    specs = [
        pl.BlockSpec((ROW_TILE, D_MODEL), lambda i: (i, 0)),
        _mod_spec(layer, 0, cond_row or 0, mod_tpb),
        _mod_spec(layer, 1, cond_row or 0, mod_tpb),
        pl.BlockSpec((1, D_MODEL), lambda i: (0, 0)),
        pl.BlockSpec((D_MODEL, width), lambda i: (0, 0)),
    ]
    if even:
        ins += [qn, kn, bd]
        specs += [pl.BlockSpec((1, LANE), lambda i: (0, 0)),
                  pl.BlockSpec((1, LANE), lambda i: (0, 0)),
                  pl.BlockSpec((LANE, LANE), lambda i: (0, 0))]
        scratch = []
    else:
        hb = ROW_TILE // HALO
        n_hb = T // HALO
        cw = conv_w.shape[1]
        ins += [x, x, conv_w]
        specs += [pl.BlockSpec((HALO, D_MODEL), lambda i: (jnp.maximum(i * hb - 1, 0), 0)),
                  pl.BlockSpec((HALO, D_MODEL), lambda i: (jnp.minimum((i + 1) * hb, n_hb - 1), 0)),
                  pl.BlockSpec((CONV_K, cw), lambda i: (0, 0))]
        scratch = [pltpu.VMEM((ROW_TILE + 2 * HALO, cw), F32)]
    if rope:
        ins += list(rope_tabs)
        specs += [pl.BlockSpec((ROW_TILE, LANE), lambda i: (i % tpb, 0))] * 3
    out_specs = [pl.BlockSpec((ROW_TILE, width), lambda i: (i, 0))]
    out_shape = [jax.ShapeDtypeStruct((T, width), F32)]
    if kv_out:
        out_specs += [pl.BlockSpec((ROW_TILE, KVW), lambda i: (i, 0))] * 2
        out_shape += [jax.ShapeDtypeStruct((T, KVW), F32)] * 2
    outs = pl.pallas_call(
        functools.partial(_inproj_kernel, even=even, rope=rope, kv_out=kv_out, tiles_per_batch=tpb),
        grid=(T // ROW_TILE,),
        in_specs=specs,
        out_specs=out_specs,
        out_shape=out_shape,
        scratch_shapes=scratch,
        compiler_params=_cparams(("parallel",), [k == 4 for k in range(len(ins))]),
        name="inproj_even" if even else "inproj_odd",
    )(*ins)
    return outs if kv_out else outs[0]


LOG2E = 1.4426950408889634
SCORE_SCALE = HD ** -0.5 * LOG2E


def _values_with_ones(v):
    lower = _iota(v.shape, 1) < HD
    return jnp.where(lower, v, 1.0).astype(BF16), jnp.where(lower, 1.0, v).astype(BF16)


def _pipelined_heads(scores, finish, lookahead, lower, g_ref, o_ref):
    heads = [(c, e) for c in range(QW // LANE) for e in range(N_KV)]
    queue = [scores(*h) for h in heads[:lookahead]]
    halves = {}
    for i, (c, e) in enumerate(heads):
        cur = queue.pop(0)
        if i + lookahead < len(heads):
            queue.append(scores(*heads[i + lookahead]))
        halves[e] = finish(c, e, *cur)
        if e == N_KV - 1:
            oc = jnp.where(lower, halves[0], halves[1])
            gated = oc * _silu(g_ref[:, c * LANE:(c + 1) * LANE])
            o_ref[:, c * LANE:(c + 1) * LANE] = gated.astype(o_ref.dtype)


def _attn_dense_kernel(*refs, has_ctx, has_sink, tq, lookahead):
    q_ref, k_ref, v_ref, g_ref = refs[:4]
    pos = 4
    if has_ctx:
        kc_ref, vc_ref = refs[pos:pos + 2]
        pos += 2
    if has_sink:
        sink_ref = refs[pos]
        pos += 1
    o_ref, kb_ref, vb0_ref, vb1_ref = refs[pos:pos + 4]

    @pl.when(pl.program_id(1) == 0)
    def _():
        kb_ref[...] = k_ref[...].astype(BF16)
        vb0_ref[...], vb1_ref[...] = _values_with_ones(v_ref[...])

    kb = kb_ref[...]
    vb = (vb0_ref[...], vb1_ref[...])
    if has_ctx:
        kcb = kc_ref[0].astype(BF16)
        vcb = _values_with_ones(vc_ref[0])
    lower = _iota((tq, LANE), 1) < HD

    def scores(c, e):
        qc = q_ref[:, c * LANE:(c + 1) * LANE] * SCORE_SCALE
        qe = jnp.where(lower if e == 0 else jnp.logical_not(lower), qc, 0.0).astype(BF16)
        return _dot_nt(qe, kb), (_dot_nt(qe, kcb) if has_ctx else None)

    def finish(c, e, s, sc):
        m = jnp.max(s, axis=-1, keepdims=True)
        if has_ctx:
            m = jnp.maximum(m, jnp.max(sc, axis=-1, keepdims=True))
        if has_sink:
            sk = sink_ref[c * N_KV + e] * LOG2E
            m = jnp.maximum(m, sk)
        o = _dot(jnp.exp2((s - m).astype(BF16)), vb[e])
        if has_ctx:
            o = o + _dot(jnp.exp2(sc - m), vcb[e])
        l = pltpu.roll(o, HD, axis=1)
        if has_sink:
            l = l + jnp.exp2(sk - m)
        return o / l

    _pipelined_heads(scores, finish, lookahead, lower, g_ref, o_ref)


def _attn_dense(z, *, n_batch, seq, tq, q_off, g_off, k_off, v_off, ctx=None, sink=None):
    T = z.shape[0]
    nq = seq // tq
    has_ctx = ctx is not None
    has_sink = sink is not None
    ins = [z, z, z, z]
    specs = [
        pl.BlockSpec((tq, QW), lambda b, i: (b * nq + i, q_off // QW)),
        pl.BlockSpec((seq, KVW), lambda b, i: (b, k_off // KVW)),
        pl.BlockSpec((seq, KVW), lambda b, i: (b, v_off // KVW)),
        pl.BlockSpec((tq, QW), lambda b, i: (b * nq + i, g_off // QW)),
    ]
    if has_ctx:
        ctx_k, ctx_v, layer = ctx
        ins += [ctx_k, ctx_v]
        specs += [pl.BlockSpec((1, None, ctx_k.shape[2], KVW), lambda b, i: (b, layer, 0, 0))] * 2
    if has_sink:
        ins.append(sink)
        specs.append(pl.BlockSpec(memory_space=pltpu.SMEM))
    return pl.pallas_call(
        functools.partial(_attn_dense_kernel, has_ctx=has_ctx, has_sink=has_sink, tq=tq,
                          lookahead=N_QH if seq <= 1024 else 2),
        grid=(n_batch, nq),
        in_specs=specs,
        out_specs=pl.BlockSpec((tq, QW), lambda b, i: (b * nq + i, 0)),
        out_shape=jax.ShapeDtypeStruct((T, QW), BF16),
        scratch_shapes=[pltpu.VMEM((seq, KVW), BF16)] * 3,
        compiler_params=_cparams(("arbitrary", "arbitrary")),
        name="attn_dense",
    )(*ins)


def _attn_band_kernel(q_ref, kp_ref, kc_ref, kn_ref, vp_ref, vc_ref, vn_ref, g_ref, kx_ref, vx_ref, sink_ref,
                      o_ref, *, seq):
    n = pl.program_id(1)
    tq = Q_BLOCK
    kcat = jnp.concatenate([kp_ref[...], kc_ref[...], kn_ref[...]], axis=0).astype(BF16)
    vcat = _values_with_ones(jnp.concatenate([vp_ref[...], vc_ref[...], vn_ref[...]], axis=0))
    kxb = kx_ref[0].astype(BF16)
    vxb = _values_with_ones(vx_ref[0])
    qpos = n * tq + _iota((tq, 3 * tq), 0)
    kpos = (n - 1) * tq + _iota((tq, 3 * tq), 1)
    valid = (jnp.abs(qpos - kpos) <= WINDOW) & (kpos >= 0) & (kpos < seq)
    lower = _iota((tq, LANE), 1) < HD

    def scores(c, e):
        qc = q_ref[:, c * LANE:(c + 1) * LANE] * SCORE_SCALE
        qe = jnp.where(lower if e == 0 else jnp.logical_not(lower), qc, 0.0).astype(BF16)
        return _dot_nt(qe, kcat), _dot_nt(qe, kxb)

    def finish(c, e, s, sx):
        s = jnp.where(valid, s, -jnp.inf)
        sk = sink_ref[c * N_KV + e] * LOG2E
        m = jnp.maximum(jnp.maximum(jnp.max(s, axis=-1, keepdims=True),
                                    jnp.max(sx, axis=-1, keepdims=True)), sk)
        o = _dot(jnp.exp2(s - m), vcat[e]) + _dot(jnp.exp2(sx - m), vxb[e])
        l = pltpu.roll(o, HD, axis=1) + jnp.exp2(sk - m)
        return o / l

    _pipelined_heads(scores, finish, N_QH, lower, g_ref, o_ref)


def _attn_band(z, ctx, sink, *, n_batch, seq):
    T = z.shape[0]
    nq = seq // Q_BLOCK
    ctx_k, ctx_v, layer = ctx
    ctx_spec = pl.BlockSpec((1, None, ctx_k.shape[2], KVW), lambda b, i: (b, layer, 0, 0))

    def kv_spec(off, delta):
        return pl.BlockSpec(
            (Q_BLOCK, KVW),
            lambda b, i: (b * nq + jnp.clip(i + delta, 0, nq - 1), off // KVW))

    specs = [
        pl.BlockSpec((Q_BLOCK, QW), lambda b, i: (b * nq + i, O_Q // QW)),
        kv_spec(O_K, -1), kv_spec(O_K, 0), kv_spec(O_K, 1),
        kv_spec(O_V, -1), kv_spec(O_V, 0), kv_spec(O_V, 1),
        pl.BlockSpec((Q_BLOCK, QW), lambda b, i: (b * nq + i, O_G // QW)),
        ctx_spec,
        ctx_spec,
        pl.BlockSpec(memory_space=pltpu.SMEM),
    ]
    return pl.pallas_call(
        functools.partial(_attn_band_kernel, seq=seq),
        grid=(n_batch, nq),
        in_specs=specs,
        out_specs=pl.BlockSpec((Q_BLOCK, QW), lambda b, i: (b * nq + i, 0)),
        out_shape=jax.ShapeDtypeStruct((T, QW), BF16),
        compiler_params=_cparams(("parallel", "parallel")),
        name="attn_band",
    )(z, z, z, z, z, z, z, z, ctx_k, ctx_v, sink)


def _gla_tile(dirs, n_chunk):
    rows = n_chunk * CHUNK
    shift = CHUNK.bit_length() - 1
    ii = _iota((rows, rows), 0)
    jj = _iota((rows, rows), 1)
    same = lax.shift_right_logical(ii, shift) == lax.shift_right_logical(jj, shift)
    sel = jnp.where(lax.shift_right_logical(_iota((rows, n_chunk * B_DV), 0), shift)
                    == lax.shift_right_logical(_iota((rows, n_chunk * B_DV), 1), B_DV.bit_length() - 1),
                    1.0, 0.0).astype(BF16)
    ci = _iota((CHUNK, CHUNK), 0)
    cj = _iota((CHUNK, CHUNK), 1)
    lower = _iota((CHUNK, LANE), 1) < B_DK
    n_pair = B_HEADS * B_DK // LANE
    per_pair = LANE // B_DK

    gs = []
    for (_, _, _, r_ref, up, bias, _, _, _) in dirs:
        gk = _dot(r_ref[...], up) + bias
        gs.append(-_softplus(-gk) * (1.0 / GLA_NORMALIZER))
    bs = []
    for g, d in zip(gs, dirs):
        tri = jnp.where(same & ((jj >= ii) if d[8] else (jj <= ii)), 1.0, 0.0).astype(BF16)
        bs.append(_dot_exact_lhs(tri, g))
    decargs = [_dot_tn_exact_rhs(g, sel) for g in gs]

    units = []
    for di, (q_ref, k_ref, v_ref, _, _, _, st_ref, o_ref, rev) in enumerate(dirs):
        b = bs[di]
        incl = (cj >= ci) if rev else (cj <= ci)
        order = range(n_chunk - 1, -1, -1) if rev else range(n_chunk)
        for c in order:
            r0 = c * CHUNK
            bc = b[r0:r0 + CHUNK]
            btot = bc[0:1, :] if rev else bc[CHUNK - 1:CHUNK, :]
            qt = q_ref[r0:r0 + CHUNK, :] * (B_DK ** -0.5) * jnp.exp(bc)
            kc = k_ref[r0:r0 + CHUNK, :]
            kt = kc * jnp.exp(-bc)
            kend = kc * jnp.exp(btot - bc)
            for p in range(n_pair):
                for e in range(per_pair):
                    h = p * per_pair + e
                    qm = jnp.where(lower if e == 0 else jnp.logical_not(lower), qt[:, p * LANE:(p + 1) * LANE], 0.0)
                    units.append(dict(di=di, c=c, p=p, e=e, incl=incl, qm=qm.astype(BF16),
                                      kt=kt[:, p * LANE:(p + 1) * LANE].astype(BF16),
                                      kend=kend[:, p * LANE:(p + 1) * LANE].astype(BF16),
                                      v=v_ref[r0:r0 + CHUNK, h * B_DV:(h + 1) * B_DV].astype(BF16)))
    for u in units:
        u["att"] = jnp.where(u["incl"], _dot_nt(u["qm"], u["kt"]), 0.0).astype(BF16)
    for u in units:
        u["upd"] = _dot_tn(u["kend"], u["v"])[u["e"] * B_DK:(u["e"] + 1) * B_DK]
    for di, d in enumerate(dirs):
        st = d[6][...]
        mine = [u for u in units if u["di"] == di]
        for k0 in range(0, len(mine), B_HEADS):
            grp = mine[k0:k0 + B_HEADS]
            c = grp[0]["c"]
            for u in grp:
                u["st"] = st[u["p"] * LANE:(u["p"] + 1) * LANE].astype(BF16)
            dec = jnp.exp(decargs[di][:, c * B_DV:(c + 1) * B_DV])
            st = dec * st + jnp.concatenate([u["upd"] for u in grp], axis=0)
        d[6][...] = st
    for u in units:
        u["qs"] = _dot(u["qm"], u["st"])
    for u in units:
        u["o"] = _dot(u["att"], u["v"]) + u["qs"]
    for di, d in enumerate(dirs):
        mine = [u for u in units if u["di"] == di]
        for k0 in range(0, len(mine), B_HEADS):
            grp = mine[k0:k0 + B_HEADS]
            r0 = grp[0]["c"] * CHUNK
            d[7][r0:r0 + CHUNK, :] = jnp.concatenate([u["o"] for u in grp], axis=1)


def _gla_kernel(*refs, has_s0, n_chunk):
    qf, kf, vf, rf, qb, kb, vb, rb, up_ref, bias_ref = refs[:10]
    pos = 10
    if has_s0:
        s0f, s0b = refs[pos:pos + 2]
        pos += 2
    of_ref, ob_ref, sf_ref, sb_ref, stf, stb = refs[pos:pos + 6]
    n = pl.program_id(1)

    @pl.when(n == 0)
    def _():
        if has_s0:
            stf[...] = s0f[0]
            stb[...] = s0b[0]
        else:
            stf[...] = jnp.zeros_like(stf)
            stb[...] = jnp.zeros_like(stb)

    _gla_tile([(qf, kf, vf, rf, up_ref[0], bias_ref[0], stf, of_ref, False),
               (qb, kb, vb, rb, up_ref[1], bias_ref[1], stb, ob_ref, True)], n_chunk)

    @pl.when(n == pl.num_programs(1) - 1)
    def _():
        sf_ref[0] = stf[...]
        sb_ref[0] = stb[...]


def _gla(z, up_ext, bias, s0, *, n_batch, seq):
    T = z.shape[0]
    nt = seq // REC_TILE
    has_s0 = s0 is not None
    sw = B_HEADS * B_DK

    def fwd(b, i):
        return b * nt + i

    def bwd(b, i):
        return b * nt + nt - 1 - i

    def specs_for(row):
        return [
            pl.BlockSpec((REC_TILE, sw), lambda b, i: (row(b, i), E_BQ // sw)),
            pl.BlockSpec((REC_TILE, sw), lambda b, i: (row(b, i), E_BK // sw)),
            pl.BlockSpec((REC_TILE, QW), lambda b, i: (row(b, i), E_BV // QW)),
            pl.BlockSpec((REC_TILE, LANE), lambda b, i: (row(b, i), E_R // LANE)),
        ]

    ins = [z] * 8 + [up_ext, bias]
    specs = specs_for(fwd) + specs_for(bwd) + [
        pl.BlockSpec((2, LANE, sw), lambda b, i: (0, 0, 0)),
        pl.BlockSpec((2, 1, sw), lambda b, i: (0, 0, 0)),
    ]
    if has_s0:
        states, layer = s0
        ins += [states, states]
        specs += [pl.BlockSpec((1, None, None, sw, B_DV), lambda b, i, d=d: (b, layer, d, 0, 0)) for d in range(2)]
    st_shape = jax.ShapeDtypeStruct((n_batch, sw, B_DV), F32)
    o_shape = jax.ShapeDtypeStruct((T, QW), F32)
    return pl.pallas_call(
        functools.partial(_gla_kernel, has_s0=has_s0, n_chunk=REC_TILE // CHUNK),
        grid=(n_batch, nt),
        in_specs=specs,
        out_specs=[
            pl.BlockSpec((REC_TILE, QW), lambda b, i: (fwd(b, i), 0)),
            pl.BlockSpec((REC_TILE, QW), lambda b, i: (bwd(b, i), 0)),
            pl.BlockSpec((1, sw, B_DV), lambda b, i: (b, 0, 0)),
            pl.BlockSpec((1, sw, B_DV), lambda b, i: (b, 0, 0)),
        ],
        out_shape=[o_shape, o_shape, st_shape, st_shape],
        scratch_shapes=[pltpu.VMEM((sw, B_DV), F32), pltpu.VMEM((sw, B_DV), F32)],
        compiler_params=_cparams(("arbitrary", "arbitrary")),
        name="gla",
    )(*ins)


def _dot3_stacked(lhs_list, rhs, expand=None):
    rh, rl = _split(rhs)
    if expand is not None:
        rh, rl = expand(rh), expand(rl)
    parts = [_split(x) for x in lhs_list]
    his = [p[0] for p in parts]
    los = [p[1] for p in parts]
    n = lhs_list[0].shape[0]
    k = len(lhs_list)
    r1 = _dot(jnp.concatenate(his + los, axis=0), rh)
    r2 = _dot(jnp.concatenate(his, axis=0) if k > 1 else his[0], rl)
    return [r1[i * n:(i + 1) * n] + (r1[(k + i) * n:(k + i + 1) * n] + r2[i * n:(i + 1) * n]) for i in range(k)]


def _delta_tile(dirs, alog, dtb, n_chunk):
    C = CHUNK
    W = D_HEADS * C
    hk = D_HEADS * D_DK
    sh = C.bit_length() - 1
    ii = _iota((C, C), 0)
    jj = _iota((C, C), 1)
    pi = _iota((C, W), 0)
    pl_ = _iota((C, W), 1)
    pj = pl_ & (C - 1)
    pblk = lax.shift_right_logical(pl_, sh)
    eye_p = jnp.where(pi == pj, 1.0, 0.0)
    bdm = (lax.shift_right_logical(_iota((W, W), 0), sh) == lax.shift_right_logical(_iota((W, W), 1), sh))
    bdk = (lax.shift_right_logical(_iota((W, hk), 0), sh)
           == lax.shift_right_logical(_iota((W, hk), 1), D_DK.bit_length() - 1))
    ones_cc = jnp.ones((C, C), BF16)

    def pack(cols):
        out = jnp.broadcast_to(cols[D_HEADS - 1], (C, W))
        for h in range(D_HEADS - 2, -1, -1):
            out = jnp.where(pblk == h, jnp.broadcast_to(cols[h], (C, W)), out)
        return out

    def wide(cols):
        return jnp.concatenate([jnp.broadcast_to(c, (C, D_DK)) for c in cols], axis=1)

    def block_diag(x):
        return jnp.where(bdm, jnp.concatenate([x] * D_HEADS, axis=0), jnp.zeros((), x.dtype))

    units = []
    for d, (x_ref, ab_ref, st_ref, o_ref, rev) in enumerate(dirs):
        incl = (jj >= ii) if rev else (jj <= ii)
        tri = jnp.where(incl, 1.0, 0.0).astype(BF16)
        incl_p = (pj >= pi) if rev else (pj <= pi)
        strict_p = (pj > pi) if rev else (pj < pi)
        before_p = (pj <= pi) if rev else (pj >= pi)
        order = range(n_chunk - 1, -1, -1) if rev else range(n_chunk)
        for c in order:
            r0 = c * C
            ab = ab_ref[r0:r0 + C, :]
            gall = -jnp.exp(alog) * _softplus(ab + dtb)
            beta_all = _sigmoid(ab)
            gc_all = _dot_exact_lhs(tri, gall)
            ig = [d * D_HEADS + h for h in range(D_HEADS)]
            gcols = [gc_all[:, i:i + 1] for i in ig]
            g_p = pack([gall[:, i:i + 1] for i in ig])
            gc_p = pack(gcols)
            gr_p = _dot_exact_lhs(ones_cc, jnp.where(before_p, g_p, 0.0))
            decay = jnp.exp(jnp.where(incl_p, gc_p - gr_p, -jnp.inf))
            gc_w = wide(gcols)
            beta_w = wide([beta_all[:, 2 * D_HEADS + i:2 * D_HEADS + i + 1] for i in ig])
            glast_w = gc_w[0:1, :] if rev else gc_w[C - 1:C, :]
            egc_w = jnp.exp(gc_w)
            q_all = x_ref[r0:r0 + C, 0:hk]
            k_all = x_ref[r0:r0 + C, hk:2 * hk]
            v_all = x_ref[r0:r0 + C, 2 * hk:3 * hk]
            kb_all = k_all * beta_w
            k_bd = jnp.where(bdk, jnp.concatenate([k_all] * D_HEADS, axis=0), 0.0)
            r = _dot_nt(jnp.concatenate([kb_all, q_all], axis=0), k_bd)
            p = -jnp.where(strict_p, r[:C] * decay, 0.0)
            vb_all = v_all * beta_w
            kbe_all = kb_all * egc_w
            rhs = jnp.concatenate(
                [jnp.concatenate([vb_all[:, h * D_DV:(h + 1) * D_DV], kbe_all[:, h * D_DK:(h + 1) * D_DK]], axis=1)
                 for h in range(D_HEADS)], axis=0)
            units.append(dict(d=d, r0=r0, st_ref=st_ref, o_ref=o_ref, p=p, acc=eye_p + p, att=r[C:] * decay,
                              rhs=rhs, qdec=q_all * egc_w, kdec=k_all * jnp.exp(glast_w - gc_w),
                              eglast=jnp.exp(glast_w)))
    per_pair = LANE // C
    half = lax.shift_right_logical(_iota((C, LANE), 1), sh)

    def square_first(us):
        for u in us:
            u["p"] = _dot3_stacked([u["p"]], u["p"], block_diag)[0]

    def square_and_extend(us):
        for u in us:
            pp, ap = _dot3_stacked([u["p"], u["acc"]], u["p"], block_diag)
            u["acc"] = u["acc"] + ap
            u["p"] = pp

    def extend_last(us):
        for u in us:
            u["acc"] = u["acc"] + _dot3_stacked([u["acc"]], u["p"], block_diag)[0]

    def solve(us):
        for u in us:
            u["sol"] = []
            for c in range(D_HEADS // per_pair):
                lanes = slice(c * LANE, (c + 1) * LANE)
                lhs = [jnp.where(half == e, u["acc"][:, lanes], 0.0) for e in range(per_pair)]
                u["sol"] += _dot3_stacked(lhs, u["rhs"][c * LANE:(c + 1) * LANE])

    n_mid = 0
    m = 2
    while 2 * m < C:
        n_mid += 1
        m *= 2
    local_stages = [square_first] + [square_and_extend] * n_mid + [extend_last, solve]

    def read_state(cur):
        for u in cur:
            u["s"] = [u["st_ref"][h] for h in range(D_HEADS)]
            u["vnew"], u["qs"] = [], []
            for h in range(D_HEADS):
                lhs = jnp.concatenate([u["sol"][h][:, D_DV:], u["qdec"][:, h * D_DK:(h + 1) * D_DK]], axis=0)
                r = _dot(lhs, u["s"][h])
                u["vnew"].append(u["sol"][h][:, :D_DV] - r[:C])
                u["qs"].append(r[C:])

    def write_state(cur):
        for u in cur:
            o_pairs = []
            for c in range(D_HEADS // per_pair):
                lanes = slice(c * LANE, (c + 1) * LANE)
                att_rows = jnp.concatenate(
                    [jnp.where(half == e, u["att"][:, lanes], 0.0) for e in range(per_pair)],
                    axis=0)
                o_pairs.append(_dot(att_rows, jnp.concatenate(u["vnew"][c * per_pair:(c + 1) * per_pair], axis=0)))
            o_all = jnp.concatenate(o_pairs, axis=0)
            outs = []
            for h in range(D_HEADS):
                outs.append(u["qs"][h] + o_all[h * C:(h + 1) * C])
                u["st_ref"][h] = (u["s"][h] * u["eglast"][:, h * D_DK:(h + 1) * D_DK]
                                  + _dot_tn(u["kdec"][:, h * D_DK:(h + 1) * D_DK], u["vnew"][h]))
            u["o_ref"][u["r0"]:u["r0"] + C, :] = jnp.concatenate(outs, axis=1)

    per_dir = [[u for u in units if u["d"] == d] for d in range(len(dirs))]
    steps = [[lst[i] for lst in per_dir] for i in range(n_chunk)]
    n_first = n_chunk // 2
    for stage in local_stages:
        stage([u for cur in steps[:n_first] for u in cur])
    pending = [functools.partial(f, cur) for cur in steps[:n_first] for f in (read_state, write_state)]
    for stage in local_stages:
        stage([u for cur in steps[n_first:] for u in cur])
        if pending:
            pending.pop(0)()
    for f in pending:
        f()
    for cur in steps[n_first:]:
        read_state(cur)
        write_state(cur)


def _delta_kernel(*refs, has_s0, n_chunk):
    xf, af, xb, ab_, alog_ref, dtb_ref = refs[:6]
    pos = 6
    if has_s0:
        s0f, s0b = refs[pos:pos + 2]
        pos += 2
    of_ref, ob_ref, sf_ref, sb_ref, stf, stb = refs[pos:pos + 6]
    n = pl.program_id(1)

    @pl.when(n == 0)
    def _():
        if has_s0:
            stf[...] = s0f[0]
            stb[...] = s0b[0]
        else:
            stf[...] = jnp.zeros_like(stf)
            stb[...] = jnp.zeros_like(stb)

    _delta_tile([(xf, af, stf, of_ref, False), (xb, ab_, stb, ob_ref, True)],
                alog_ref[...], dtb_ref[...], n_chunk)

    @pl.when(n == pl.num_programs(1) - 1)
    def _():
        sf_ref[0] = stf[...]
        sb_ref[0] = stb[...]


def _delta(z, alog_row, dtb_row, s0, *, n_batch, seq):
    T = z.shape[0]
    nt = seq // REC_TILE
    has_s0 = s0 is not None
    cw = 3 * D_HEADS * D_DK
    assert O_DQKV == 0 and ROW_TILE == REC_TILE

    def fwd(b, i):
        return b * nt + i

    def bwd(b, i):
        return b * nt + nt - 1 - i

    ins = [z, z, z, z, alog_row, dtb_row]
    specs = [
        pl.BlockSpec((REC_TILE, cw), lambda b, i: (fwd(b, i), 0)),
        pl.BlockSpec((REC_TILE, LANE), lambda b, i: (fwd(b, i), O_AB // LANE)),
        pl.BlockSpec((REC_TILE, cw), lambda b, i: (bwd(b, i), 0)),
        pl.BlockSpec((REC_TILE, LANE), lambda b, i: (bwd(b, i), O_AB // LANE)),
        pl.BlockSpec((1, LANE), lambda b, i: (0, 0)),
        pl.BlockSpec((1, LANE), lambda b, i: (0, 0)),
    ]
    st_block = (1, D_HEADS, D_DK, D_DV)
    if has_s0:
        states, layer = s0
        ins += [states, states]
        specs += [pl.BlockSpec((1, None, None, D_HEADS, D_DK, D_DV), lambda b, i, d=d: (b, layer, d, 0, 0, 0))
                  for d in range(2)]
    st_shape = jax.ShapeDtypeStruct((n_batch, D_HEADS, D_DK, D_DV), F32)
    o_shape = jax.ShapeDtypeStruct((T, QW), F32)
    return pl.pallas_call(
        functools.partial(_delta_kernel, has_s0=has_s0, n_chunk=REC_TILE // CHUNK),
        grid=(n_batch, nt),
        in_specs=specs,
        out_specs=[
            pl.BlockSpec((REC_TILE, QW), lambda b, i: (fwd(b, i), 0)),
            pl.BlockSpec((REC_TILE, QW), lambda b, i: (bwd(b, i), 0)),
            pl.BlockSpec(st_block, lambda b, i: (b, 0, 0, 0)),
            pl.BlockSpec(st_block, lambda b, i: (b, 0, 0, 0)),
        ],
        out_shape=[o_shape, o_shape, st_shape, st_shape],
        scratch_shapes=[pltpu.VMEM((D_HEADS, D_DK, D_DV), F32), pltpu.VMEM((D_HEADS, D_DK, D_DV), F32)],
        compiler_params=_cparams(("arbitrary", "arbitrary")),
        name="delta",
    )(*ins)


def _outproj_kernel(ya_ref, of_ref, ob_ref, zg_ref, on_ref, w_ref, x_ref, gate_ref, gp_ref, o_ref):
    ob = of_ref[...] + ob_ref[...]
    zg = zg_ref[...]
    on = on_ref[...]
    parts = []
    for h in range(QW // LANE):
        seg = ob[:, h * LANE:(h + 1) * LANE]
        nrm = seg * lax.rsqrt(jnp.mean(seg * seg, axis=-1, keepdims=True) + EPS) * on
        parts.append(nrm * _silu(zg[:, h * LANE:(h + 1) * LANE]))
    yb = jnp.concatenate(parts, axis=1)
    out = _dot(ya_ref[...], w_ref[0:QW, :]) + _dot(yb, w_ref[QW:, :])
    post = out * lax.rsqrt(jnp.mean(out * out, axis=-1, keepdims=True) + EPS) * gp_ref[...]
    o_ref[...] = x_ref[...] + gate_ref[0] * post


def _outproj(ya, o_f, o_b, z, g_off, onorm, w_out, x, mod, layer, cond_row, g_post, *, rows_per_batch):
    T = x.shape[0]
    row = lambda i: (i, 0)
    gate_spec = _mod_spec(layer, 2, cond_row or 0, None if cond_row is None else rows_per_batch // OUT_TILE)
    return pl.pallas_call(
        _outproj_kernel,
        grid=(T // OUT_TILE,),
        in_specs=[
            pl.BlockSpec((OUT_TILE, QW), row),
            pl.BlockSpec((OUT_TILE, QW), row),
            pl.BlockSpec((OUT_TILE, QW), row),
            pl.BlockSpec((OUT_TILE, QW), lambda i: (i, g_off // QW)),
            pl.BlockSpec((1, LANE), lambda i: (0, 0)),
            pl.BlockSpec((2 * QW, D_MODEL), lambda i: (0, 0)),
            pl.BlockSpec((OUT_TILE, D_MODEL), row),
            gate_spec,
            pl.BlockSpec((1, D_MODEL), lambda i: (0, 0)),
        ],
        out_specs=pl.BlockSpec((OUT_TILE, D_MODEL), row),
        out_shape=jax.ShapeDtypeStruct((T, D_MODEL), F32),
        compiler_params=_cparams(("parallel",)),
        name="outproj",
    )(ya, o_f, o_b, z, onorm.reshape(1, LANE), w_out, x, mod, g_post.reshape(1, D_MODEL))


def _rope_tables(seq):
    n_rows = seq // GRID_W
    rows = jnp.repeat(jnp.arange(n_rows, dtype=F32), GRID_W)
    cols = jnp.tile(jnp.arange(GRID_W, dtype=F32), n_rows)
    inv = jnp.power(ROPE_THETA, jnp.arange(ROT_AXIS // 2, dtype=F32) * (-2.0 / ROT_AXIS))
    ang_r = rows[:, None] * inv[None, :]
    ang_c = cols[:, None] * inv[None, :]
    zero = jnp.zeros_like(ang_r)
    cos = jnp.concatenate([jnp.cos(ang_r)] * 2 + [jnp.cos(ang_c)] * 2, axis=1)
    sa = jnp.concatenate([-jnp.sin(ang_r), zero, -jnp.sin(ang_c), zero], axis=1)
    sb = jnp.concatenate([zero, jnp.sin(ang_r), zero, jnp.sin(ang_c)], axis=1)
    return tuple(jnp.tile(t, (1, LANE // HD)) for t in (cos, sa, sb))


def _regroup_cols(w):
    g = N_QH // N_KV
    lead = w.shape[:-1]
    return jnp.swapaxes(w.reshape(lead + (N_KV, g, HD)), -3, -2).reshape(lead + (QW,))


def _regroup_rows(w):
    g = N_QH // N_KV
    return jnp.swapaxes(w.reshape(w.shape[:-2] + (N_KV, g, HD, w.shape[-1])), -4, -3).reshape(w.shape)


def _prep_even_weights(w_in, w_out, gk_up):
    n = w_in.shape[0]
    w_in = w_in.astype(BF16)
    a_q, a_k, a_v, a_g, b_q, b_k, b_v, b_r, b_g = jnp.split(
        w_in, [512, 640, 768, 1280, 1536, 1792, 2304, 2336], axis=2)
    pad = jnp.zeros((n, D_MODEL, LANE - 2 * B_RANK), BF16)
    w = jnp.concatenate([_regroup_cols(a_q), _regroup_cols(a_g), b_v, b_g, b_q, b_k, a_k, a_v, b_r, pad], axis=2)
    w_out = w_out.astype(BF16)
    wo = jnp.concatenate([_regroup_rows(w_out[:, :QW]), w_out[:, QW:]], axis=1)
    zeros = jnp.zeros((n, B_RANK, B_HEADS * B_DK), BF16)
    tail = jnp.zeros((n, LANE - 2 * B_RANK, B_HEADS * B_DK), BF16)
    gk_up = gk_up.astype(BF16)
    up = jnp.stack([jnp.concatenate([gk_up[:, 0], zeros, tail], axis=1),
                    jnp.concatenate([zeros, gk_up[:, 1], tail], axis=1)], axis=1)
    return w, wo, up


def _prep_odd_weights(w_in, w_out):
    n = w_in.shape[0]
    w_in = w_in.astype(BF16)
    c_q, c_k, c_v, c_g, d_q, d_k, d_v, d_a, d_b, d_g = jnp.split(
        w_in, [512, 640, 768, 1280, 1792, 2304, 2816, 2824, 2832], axis=2)
    pad = jnp.zeros((n, D_MODEL, LANE - 4 * D_HEADS), BF16)
    w = jnp.concatenate([d_q, d_k, d_v, _regroup_cols(c_q), _regroup_cols(c_g), d_g, c_k, c_v, d_a, d_b, pad],
                        axis=2)
    w_out = w_out.astype(BF16)
    wo = jnp.concatenate([_regroup_rows(w_out[:, :QW]), w_out[:, QW:]], axis=1)
    return w, wo


def _lane_row(x):
    flat = x.reshape(-1).astype(F32)
    return jnp.zeros((1, LANE), F32).at[0, :flat.shape[0]].set(flat)


def kernel(x_prompt, x_sample, c, cache_attn_k, cache_attn_v, state_gla, cache_swa_k, cache_swa_v, state_delta,
           c_ctx, w_mod, b_mod, g_pre, g_post, w_in_even, w_out_even, qnorm_a, knorm_a, gla_gk_up, gla_gk_bias,
           gla_onorm, w_in_odd, w_out_odd, sink_c, conv_d, a_log_d, dt_bias_d, delta_onorm):
    n_p, seq_p, _ = x_prompt.shape
    n_s, seq_s, _ = x_sample.shape
    n_ctx = cache_attn_k.shape[2]
    assert (n_p * seq_p) % OUT_TILE == 0 and seq_s % OUT_TILE == 0 and seq_p % ROW_TILE == 0
    assert seq_s % ROW_TILE == 0 and seq_p % REC_TILE == 0 and seq_s % REC_TILE == 0
    assert n_s + 1 <= N_COND

    cond = jnp.zeros((N_COND, D_MODEL), F32).at[0].set(c_ctx).at[1:1 + n_s].set(c)
    mod = _modulation(cond, w_mod, b_mod)
    rope = _rope_tables(seq_s)
    bd = jnp.asarray(np.kron(np.eye(LANE // HD), np.ones((HD, HD))), BF16)
    w_even, wo_even, up_even = _prep_even_weights(w_in_even, w_out_even, gla_gk_up)
    w_odd, wo_odd = _prep_odd_weights(w_in_odd, w_out_odd)
    sinks = jnp.swapaxes(sink_c.astype(F32).reshape(-1, N_KV, N_QH // N_KV), 1, 2).reshape(-1, N_QH)
    ctx_attn_k, ctx_attn_v, ctx_swa_k, ctx_swa_v = [
        a.reshape(a.shape[0], a.shape[1], n_ctx, KVW) for a in (cache_attn_k, cache_attn_v, cache_swa_k, cache_swa_v)]
    gla_states = state_gla.reshape(n_s, -1, 2, B_HEADS * B_DK, B_DV)

    y_p = x_prompt.reshape(n_p * seq_p, D_MODEL)
    y_s = x_sample.reshape(n_s * seq_s, D_MODEL)
    attn_k, attn_v, gla_s, swa_k, swa_v, delta_s = [], [], [], [], [], []
    for l in range(DEPTH):
        i = l // 2
        if l % 2 == 0:
            w, wo, up = w_even[i], wo_even[i], up_even[i]
            qn = jnp.tile(qnorm_a[i], LANE // HD).reshape(1, LANE)
            kn = jnp.tile(knorm_a[i], LANE // HD).reshape(1, LANE)
            bias = gla_gk_bias[i].reshape(2, 1, B_HEADS * B_DK)
            common = dict(even=True, qn=qn, kn=kn, bd=bd)
            z_p, k_new, v_new = _inproj(y_p, mod, l, None, g_pre[l], w, rows_per_batch=seq_p, kv_out=True, **common)
            z_s = _inproj(y_s, mod, l, 1, g_pre[l], w, rows_per_batch=seq_s, rope_tabs=rope, **common)
            offs = dict(q_off=E_Q, g_off=E_G, k_off=E_K, v_off=E_V)
            ya_p = _attn_dense(z_p, n_batch=n_p, seq=seq_p, tq=seq_p, **offs)
            ctx = (ctx_attn_k, ctx_attn_v, i)
            ya_s = _attn_dense(z_s, n_batch=n_s, seq=seq_s, tq=256, ctx=ctx, **offs)
            of_p, ob_p, sf_p, sb_p = _gla(z_p, up, bias, None, n_batch=n_p, seq=seq_p)
            of_s, ob_s, _, _ = _gla(z_s, up, bias, (gla_states, i), n_batch=n_s, seq=seq_s)
            attn_k.append(k_new.reshape(n_p, seq_p, N_KV, HD))
            attn_v.append(v_new.reshape(n_p, seq_p, N_KV, HD))
            gla_s.append(jnp.stack([sf_p, sb_p], axis=1).reshape(n_p, 2, B_HEADS, B_DK, B_DV))
            g_off, onorm = E_BG, gla_onorm[i]
        else:
            w, wo = w_odd[i], wo_odd[i]
            z_p, k_new, v_new = _inproj(y_p, mod, l, None, g_pre[l], w, even=False, conv_w=conv_d[i],
                                        kv_out=True, rows_per_batch=seq_p)
            z_s = _inproj(y_s, mod, l, 1, g_pre[l], w, even=False, conv_w=conv_d[i], rows_per_batch=seq_s,
                          rope_tabs=rope)
            sink = sinks[i]
            ya_p = _attn_dense(z_p, n_batch=n_p, seq=seq_p, tq=seq_p, q_off=O_Q, g_off=O_G, k_off=O_K,
                               v_off=O_V, sink=sink)
            ctx = (ctx_swa_k, ctx_swa_v, i)
            ya_s = _attn_band(z_s, ctx, sink, n_batch=n_s, seq=seq_s)
            alog_row = _lane_row(a_log_d[i])
            dtb_row = _lane_row(dt_bias_d[i])
            of_p, ob_p, sf_p, sb_p = _delta(z_p, alog_row, dtb_row, None, n_batch=n_p, seq=seq_p)
            of_s, ob_s, _, _ = _delta(z_s, alog_row, dtb_row, (state_delta, i), n_batch=n_s, seq=seq_s)
            swa_k.append(k_new.reshape(n_p, seq_p, N_KV, HD))
            swa_v.append(v_new.reshape(n_p, seq_p, N_KV, HD))
            delta_s.append(jnp.stack([sf_p, sb_p], axis=1))
            g_off, onorm = O_DG, delta_onorm[i]
        y_p = _outproj(ya_p, of_p, ob_p, z_p, g_off, onorm, wo, y_p, mod, l, None, g_post[l], rows_per_batch=seq_p)
        y_s = _outproj(ya_s, of_s, ob_s, z_s, g_off, onorm, wo, y_s, mod, l, 1, g_post[l], rows_per_batch=seq_s)
    return (y_p.reshape(n_p, seq_p, D_MODEL), y_s.reshape(n_s, seq_s, D_MODEL),
            jnp.stack(attn_k, axis=1), jnp.stack(attn_v, axis=1), jnp.stack(gla_s, axis=1),
            jnp.stack(swa_k, axis=1), jnp.stack(swa_v, axis=1), jnp.stack(delta_s, axis=1))
```
